```python
import math
import jax, jax.numpy as jnp
from jax import lax
import numpy as np

D_MODEL = 2048
BATCH = 1
SEQ = 8192
DEPTH = 4

GRID_W = 64
CTX_LEN = 256
ROPE_THETA = 10000.0
ROPE_DIM = 64
Q_BLOCK = 128
NEG_INF = -1e30
EPS = 1e-6

A_HEADS = 4
A_QK_DIM = 64
A_V_DIM = 2 * A_QK_DIM
B_HEADS = 4
B_Q_LORA = 512
B_KV_LORA = 256
B_NOPE = 128
B_ROPE = 64
B_V = 128
C_HEADS = 8
C_KV_HEADS = 2
C_GROUP = C_HEADS // C_KV_HEADS
C_HEAD_DIM = 64
C_WINDOW = 128
C_BLOCK = 128
D_HEADS = 4
D_HEAD_DIM = 128
NA_ROWS = 8
NA_COLS = 16

N_BRANCHES = 4
BRANCH_WIDTH = 512
A_IN = A_HEADS * (4 * A_QK_DIM + A_V_DIM)
B_IN = B_Q_LORA + B_KV_LORA + B_ROPE
C_IN = (C_HEADS + 2 * C_KV_HEADS) * C_HEAD_DIM
D_IN = 3 * D_HEADS * D_HEAD_DIM
IN_WIDTH = A_IN + B_IN + C_IN + D_IN
IN_SPLITS = (A_IN, A_IN + B_IN, A_IN + B_IN + C_IN)

PEER_HEADS = 8
PEER_N_KEYS = 128
PEER_N_EXPERTS = PEER_N_KEYS ** 2
PEER_KEY_DIM = 256
PEER_TOPK = 16
PEER_BLOCK = 128

kernel_name = 'hybrid_gated_mixers_peer_dit'


def rms_norm(x, g):
    xf = x.astype(jnp.float32)
    y = xf * lax.rsqrt(jnp.mean(xf * xf, axis=-1, keepdims=True) + EPS)
    return (y * g.astype(jnp.float32)).astype(x.dtype)


def modulate(h, shift, scale):
    return h * (1.0 + scale) + shift


def axial_rope(n_tokens):
    t = jnp.arange(n_tokens, dtype=jnp.int32)
    row = (t // GRID_W).astype(jnp.float32)
    col = (t % GRID_W).astype(jnp.float32)
    n_freq = ROPE_DIM // 4
    inv = ROPE_THETA ** (-jnp.arange(n_freq, dtype=jnp.float32) / n_freq)
    ang = jnp.concatenate([row[:, None] * inv, col[:, None] * inv], axis=-1)
    return jnp.cos(ang), jnp.sin(ang)


def apply_rope(x, cos, sin):
    shape = (1, cos.shape[0]) + (1,) * (x.ndim - 3) + (cos.shape[1],)
    cs, sn = cos.reshape(shape), sin.reshape(shape)
    x1, x2 = jnp.split(x.astype(jnp.float32), 2, axis=-1)
    return jnp.concatenate([x1 * cs - x2 * sn, x1 * sn + x2 * cs], axis=-1).astype(x.dtype)


def softmax_attend(q, k, v, scale):
    s = jnp.einsum('bqhd,bkhd->bhqk', q, k).astype(jnp.float32) * scale
    p = jax.nn.softmax(s, axis=-1).astype(v.dtype)
    return jnp.einsum('bhqk,bkhd->bqhd', p, v)


def softmax_with_sink(s, sink):
    sink = jnp.broadcast_to(sink, s.shape[:-1] + (1,))
    p = jax.nn.softmax(jnp.concatenate([s, sink], axis=-1), axis=-1)
    return p[..., :-1]


def sweep_query_blocks(attend, q):
    b, n = q.shape[:2]
    nb = n // Q_BLOCK
    qb = jnp.moveaxis(q.reshape((b, nb, Q_BLOCK) + q.shape[2:]), 1, 0)
    o = lax.map(attend, qb)
    return jnp.moveaxis(o, 0, 1).reshape((b, n) + o.shape[3:])


def diff_attention(pa_lat, pa_ctx, cos, sin, lam_q1, lam_k1, lam_q2, lam_k2, subln_g, lam_init, with_ctx):
    def heads(p):
        b, n = p.shape[:2]
        q, k, v = jnp.split(p, (2 * A_HEADS * A_QK_DIM, 4 * A_HEADS * A_QK_DIM), axis=-1)
        return (q.reshape(b, n, A_HEADS, 2, A_QK_DIM), k.reshape(b, n, A_HEADS, 2, A_QK_DIM),
                v.reshape(b, n, A_HEADS, A_V_DIM))
    ql, kl, vl = heads(pa_lat)
    qc, kc, vc = heads(pa_ctx)
    ql, kl = apply_rope(ql, cos, sin), apply_rope(kl, cos, sin)
    f32 = jnp.float32
    lam = (jnp.exp(jnp.sum(lam_q1.astype(f32) * lam_k1.astype(f32)))
           - jnp.exp(jnp.sum(lam_q2.astype(f32) * lam_k2.astype(f32))) + lam_init)
    scale = A_QK_DIM ** -0.5

    def attend(q, k, v):
        s = jnp.einsum('bqhmd,bkhmd->bhmqk', q, k).astype(f32) * scale
        p = jax.nn.softmax(s, axis=-1)
        a = (p[:, :, 0] - lam * p[:, :, 1]).astype(v.dtype)
        return jnp.einsum('bhqk,bkhd->bqhd', a, v)

    def finish(o):
        b, n = o.shape[:2]
        return (rms_norm(o, subln_g) * (1.0 - lam_init)).reshape(b, n, A_HEADS * A_V_DIM)

    k_all = jnp.concatenate([kl, kc], axis=1)
    v_all = jnp.concatenate([vl, vc], axis=1)
    o_lat = finish(sweep_query_blocks(lambda qb: attend(qb, k_all, v_all), ql))
    o_ctx = finish(attend(qc, kc, vc)) if with_ctx else None
    return o_lat, o_ctx


def latent_attention(pb_lat, pb_ctx, cos, sin, qa_norm, w_uq, kva_norm, w_ukv, with_ctx):
    scale = (B_NOPE + B_ROPE) ** -0.5

    def queries(p, rotate):
        b, n = p.shape[:2]
        q = (rms_norm(p[..., :B_Q_LORA], qa_norm) @ w_uq).reshape(b, n, B_HEADS, B_NOPE + B_ROPE)
        q_nope, q_pe = jnp.split(q, (B_NOPE,), axis=-1)
        if rotate:
            q_pe = apply_rope(q_pe, cos, sin)
        return jnp.concatenate([q_nope, q_pe], axis=-1)

    def keys_values(p, rotate):
        b, n = p.shape[:2]
        c_kv = rms_norm(p[..., B_Q_LORA:B_Q_LORA + B_KV_LORA], kva_norm)
        kv = (c_kv @ w_ukv).reshape(b, n, B_HEADS, B_NOPE + B_V)
        k_nope, v = jnp.split(kv, (B_NOPE,), axis=-1)
        k_pe = p[..., B_Q_LORA + B_KV_LORA:][:, :, None, :]
        if rotate:
            k_pe = apply_rope(k_pe, cos, sin)
        k = jnp.concatenate([k_nope, jnp.broadcast_to(k_pe, (b, n, B_HEADS, B_ROPE))], axis=-1)
        return k, v

    b, n = pb_lat.shape[:2]
    kl, vl = keys_values(pb_lat, True)
    kc, vc = keys_values(pb_ctx, False)
    k_all = jnp.concatenate([kl, kc], axis=1)
    v_all = jnp.concatenate([vl, vc], axis=1)
    ql = queries(pb_lat, True)
    o_lat = sweep_query_blocks(lambda qb: softmax_attend(qb, k_all, v_all, scale), ql).reshape(b, n, B_HEADS * B_V)
    o_ctx = None
    if with_ctx:
        nc = pb_ctx.shape[1]
        o_ctx = softmax_attend(queries(pb_ctx, False), kc, vc, scale).reshape(pb_ctx.shape[0], nc, B_HEADS * B_V)
    return o_lat, o_ctx


def window_attention(pc_lat, pc_ctx, cos, sin, sink, with_ctx):
    def heads(p):
        b, n = p.shape[:2]
        q, k, v = jnp.split(p, (C_HEADS * C_HEAD_DIM, (C_HEADS + C_KV_HEADS) * C_HEAD_DIM), axis=-1)
        return (q.reshape(b, n, C_KV_HEADS, C_GROUP, C_HEAD_DIM), k.reshape(b, n, C_KV_HEADS, C_HEAD_DIM),
                v.reshape(b, n, C_KV_HEADS, C_HEAD_DIM))
    f32 = jnp.float32
    ql, kl, vl = heads(pc_lat)
    qc, kc, vc = heads(pc_ctx)
    ql, kl = apply_rope(ql, cos, sin), apply_rope(kl, cos, sin)
    b, n = pc_lat.shape[:2]
    nb = n // C_BLOCK
    nk = 3 * C_BLOCK
    scale = C_HEAD_DIM ** -0.5
    sink = sink.astype(f32).reshape(C_KV_HEADS, C_GROUP, 1, 1)

    def band(t):
        tp = jnp.pad(t, ((0, 0), (C_BLOCK, C_BLOCK), (0, 0), (0, 0))).reshape(b, nb + 2, C_BLOCK, C_KV_HEADS, C_HEAD_DIM)
        return jnp.concatenate([tp[:, :-2], tp[:, 1:-1], tp[:, 2:]], axis=2)

    kb, vb = band(kl), band(vl)
    qb = ql.reshape(b, nb, C_BLOCK, C_KV_HEADS, C_GROUP, C_HEAD_DIM)
    qi = jnp.arange(C_BLOCK)[:, None]
    kj = jnp.arange(nk)[None, :]
    key_pos = jnp.arange(nb)[:, None, None] * C_BLOCK + kj[None] - C_BLOCK
    in_band = jnp.abs(qi + C_BLOCK - kj) <= C_WINDOW
    valid = in_band[None] & (key_pos >= 0) & (key_pos < n)
    s_band = jnp.einsum('bnqhgd,bnkhd->bnhgqk', qb, kb).astype(f32) * scale
    s_band = jnp.where(valid[None, :, None, None], s_band, NEG_INF)
    s_ctx = jnp.einsum('bnqhgd,bkhd->bnhgqk', qb, kc).astype(f32) * scale
    p = softmax_with_sink(jnp.concatenate([s_band, s_ctx], axis=-1), sink)
    o = (jnp.einsum('bnhgqk,bnkhd->bnqhgd', p[..., :nk].astype(vb.dtype), vb)
         + jnp.einsum('bnhgqk,bkhd->bnqhgd', p[..., nk:].astype(vc.dtype), vc))
    o_lat = o.reshape(b, n, C_HEADS * C_HEAD_DIM)
    o_ctx = None
    if with_ctx:
        s = jnp.einsum('bqhgd,bkhd->bhgqk', qc, kc).astype(f32) * scale
        pc = softmax_with_sink(s, sink).astype(vc.dtype)
        o_ctx = jnp.einsum('bhgqk,bkhd->bqhgd', pc, vc).reshape(b, qc.shape[1], C_HEADS * C_HEAD_DIM)
    return o_lat, o_ctx


def neighbourhood_attention(pd_lat, pd_ctx, rpb, with_ctx):
    def heads(p):
        b, n = p.shape[:2]
        return tuple(t.reshape(b, n, D_HEADS, D_HEAD_DIM) for t in jnp.split(p, 3, axis=-1))
    f32 = jnp.float32
    ql, kl, vl = heads(pd_lat)
    qc, kc, vc = heads(pd_ctx)
    b, n = pd_lat.shape[:2]
    rows = n // GRID_W
    kr = min(NA_ROWS, rows)
    n_nb = kr * NA_COLS
    scale = D_HEAD_DIM ** -0.5
    r_ids = jnp.arange(rows)
    row_start = jnp.clip(r_ids - kr // 2, 0, rows - kr)
    col = jnp.arange(GRID_W)
    col_idx = jnp.clip(col - NA_COLS // 2, 0, GRID_W - NA_COLS)[:, None] + jnp.arange(NA_COLS)[None]
    col_bias = col_idx - col[:, None] + NA_COLS - 1
    kg = kl.reshape(b, rows, GRID_W, D_HEADS, D_HEAD_DIM)
    vg = vl.reshape(b, rows, GRID_W, D_HEADS, D_HEAD_DIM)
    qg = jnp.moveaxis(ql.reshape(b, rows, GRID_W, D_HEADS, D_HEAD_DIM), 1, 0)
    rpb = rpb.astype(f32)

    def row_block(args):
        r_i, q_row = args
        rs = row_start[r_i]
        k_nb = lax.dynamic_slice_in_dim(kg, rs, kr, axis=1)[:, :, col_idx]
        v_nb = lax.dynamic_slice_in_dim(vg, rs, kr, axis=1)[:, :, col_idx]
        row_bias = rs + jnp.arange(kr) - r_i + NA_ROWS - 1
        bias = rpb[:, row_bias[:, None, None], col_bias[None]]
        s_nb = jnp.einsum('bchd,bicjhd->bhcij', q_row, k_nb).astype(f32) * scale + jnp.transpose(bias, (0, 2, 1, 3))[None]
        s_ctx = jnp.einsum('bchd,bkhd->bhck', q_row, kc).astype(f32) * scale
        p = jax.nn.softmax(jnp.concatenate([s_nb.reshape(b, D_HEADS, GRID_W, n_nb), s_ctx], axis=-1), axis=-1)
        p_nb = p[..., :n_nb].reshape(b, D_HEADS, GRID_W, kr, NA_COLS).astype(v_nb.dtype)
        return (jnp.einsum('bhcij,bicjhd->bchd', p_nb, v_nb)
                + jnp.einsum('bhck,bkhd->bchd', p[..., n_nb:].astype(vc.dtype), vc))

    o = lax.map(row_block, (r_ids, qg))
    o_lat = jnp.moveaxis(o, 0, 1).reshape(b, n, D_HEADS * D_HEAD_DIM)
    o_ctx = None
    if with_ctx:
        o_ctx = softmax_attend(qc, kc, vc, scale).reshape(b, qc.shape[1], D_HEADS * D_HEAD_DIM)
    return o_lat, o_ctx


def gated_merge(h, branches, w_branch, w_gate, b_gate, w_out):
    merged = jax.nn.sigmoid(h @ w_gate[0] + b_gate[0]) * (branches[0] @ w_branch[0])
    for i in range(1, N_BRANCHES):
        merged = merged + jax.nn.sigmoid(h @ w_gate[i] + b_gate[i]) * (branches[i] @ w_branch[i])
    return merged @ w_out


def peer(h, w_q, sub_keys, u_tab, v_tab):
    b, n, d = h.shape
    hb = h.reshape(b * n // PEER_BLOCK, PEER_BLOCK, d)

    def block(hx):
        q = (hx @ w_q).reshape(PEER_BLOCK, PEER_HEADS, 2, PEER_KEY_DIM // 2)
        s = jnp.einsum('thpd,hpnd->thpn', q, sub_keys).astype(jnp.float32)
        half_s, half_i = lax.top_k(s, PEER_TOPK)
        cand_s = (half_s[:, :, 0, :, None] + half_s[:, :, 1, None, :]).reshape(PEER_BLOCK, PEER_HEADS, PEER_TOPK ** 2)
        cand_i = (half_i[:, :, 0, :, None] * PEER_N_KEYS + half_i[:, :, 1, None, :]).reshape(PEER_BLOCK, PEER_HEADS, PEER_TOPK ** 2)
        top_s, pos = lax.top_k(cand_s, PEER_TOPK)
        experts = jnp.take_along_axis(cand_i, pos, axis=-1)
        g = jax.nn.softmax(top_s, axis=-1)
        act = jax.nn.gelu(jnp.einsum('td,thkd->thk', hx, u_tab[experts]).astype(jnp.float32), approximate=False)
        return jnp.einsum('thk,thkd->td', (g * act).astype(hx.dtype), v_tab[experts])

    return lax.map(block, hb).reshape(b, n, d)


def setup_inputs(seed: int = 0) -> dict:
    key = jax.random.key(seed)
    ks = jax.random.split(key, 32)
    f32 = jnp.float32
    D = D_MODEL

    def nrm(k, shape, scale):
        return scale * jax.random.normal(k, shape, f32)

    def gain(k, shape):
        return 1.0 + 0.01 * jax.random.normal(k, shape, f32)

    return {
        'x': nrm(ks[0], (BATCH, SEQ, D), 1.0),
        'c': nrm(ks[1], (BATCH, D), 1.0),
        'ctx': nrm(ks[2], (BATCH, CTX_LEN, D), 1.0),
        'c_ctx': nrm(ks[3], (D,), 1.0),
        'ada_w': nrm(ks[4], (DEPTH, D, 6 * D), 0.5 * D ** -0.5),
        'ada_b': nrm(ks[5], (DEPTH, 6 * D), 0.02),
        'norm_mix': gain(ks[6], (DEPTH, D)),
        'norm_ffn': gain(ks[7], (DEPTH, D)),
        'w_in': nrm(ks[8], (DEPTH, D, IN_WIDTH), D ** -0.5),
        'a_lam_q1': nrm(ks[9], (DEPTH, A_QK_DIM), 0.1),
        'a_lam_k1': nrm(ks[10], (DEPTH, A_QK_DIM), 0.1),
        'a_lam_q2': nrm(ks[11], (DEPTH, A_QK_DIM), 0.1),
        'a_lam_k2': nrm(ks[12], (DEPTH, A_QK_DIM), 0.1),
        'a_subln': gain(ks[13], (DEPTH, A_V_DIM)),
        'b_qa_norm': gain(ks[14], (DEPTH, B_Q_LORA)),
        'b_w_uq': nrm(ks[15], (DEPTH, B_Q_LORA, B_HEADS * (B_NOPE + B_ROPE)), B_Q_LORA ** -0.5),
        'b_kva_norm': gain(ks[16], (DEPTH, B_KV_LORA)),
        'b_w_ukv': nrm(ks[17], (DEPTH, B_KV_LORA, B_HEADS * (B_NOPE + B_V)), B_KV_LORA ** -0.5),
        'c_sink': nrm(ks[18], (DEPTH, C_HEADS), 0.5),
        'd_rpb': nrm(ks[19], (DEPTH, D_HEADS, 2 * NA_ROWS - 1, 2 * NA_COLS - 1), 0.05),
        'w_branch': nrm(ks[20], (DEPTH, N_BRANCHES, BRANCH_WIDTH, D), BRANCH_WIDTH ** -0.5),
        'w_gate': nrm(ks[21], (DEPTH, N_BRANCHES, D, D), D ** -0.5),
        'b_gate': nrm(ks[22], (DEPTH, N_BRANCHES, D), 0.02),
        'w_out': nrm(ks[23], (DEPTH, D, D), D ** -0.5),
        'peer_wq': nrm(ks[24], (DEPTH, D, PEER_HEADS * PEER_KEY_DIM), D ** -0.5),
        'peer_keys': nrm(ks[25], (DEPTH, PEER_HEADS, 2, PEER_N_KEYS, PEER_KEY_DIM // 2), (PEER_KEY_DIM // 2) ** -0.5),
        'peer_u': nrm(ks[26], (DEPTH, PEER_N_EXPERTS, D), D ** -0.5),
        'peer_v': nrm(ks[27], (DEPTH, PEER_N_EXPERTS, D), PEER_HEADS ** -0.5),
        'final_norm': gain(ks[28], (D,)),
    }


def reference(x, c, ctx, c_ctx, ada_w, ada_b, norm_mix, norm_ffn, w_in, a_lam_q1, a_lam_k1, a_lam_q2, a_lam_k2,
              a_subln, b_qa_norm, b_w_uq, b_kva_norm, b_w_ukv, c_sink, d_rpb, w_branch, w_gate, b_gate, w_out,
              peer_wq, peer_keys, peer_u, peer_v, final_norm):
    cos, sin = axial_rope(x.shape[1])
    xc = ctx
    for l in range(DEPTH):
        with_ctx = l < DEPTH - 1
        lam_init = 0.8 - 0.6 * math.exp(-0.3 * l)
        mod_lat = (jax.nn.silu(c) @ ada_w[l] + ada_b[l])[:, None, :]
        mod_ctx = (jax.nn.silu(c_ctx) @ ada_w[l] + ada_b[l])[None, None, :]
        sh1, sc1, g1, sh2, sc2, g2 = jnp.split(mod_lat, 6, axis=-1)
        sh1c, sc1c, g1c, sh2c, sc2c, g2c = jnp.split(mod_ctx, 6, axis=-1)

        h_lat = modulate(rms_norm(x, norm_mix[l]), sh1, sc1)
        h_ctx = modulate(rms_norm(xc, norm_mix[l]), sh1c, sc1c)
        pa, pb, pc, pd = jnp.split(h_lat @ w_in[l], IN_SPLITS, axis=-1)
        pac, pbc, pcc, pdc = jnp.split(h_ctx @ w_in[l], IN_SPLITS, axis=-1)
        oa, oac = diff_attention(pa, pac, cos, sin, a_lam_q1[l], a_lam_k1[l], a_lam_q2[l], a_lam_k2[l],
                                 a_subln[l], lam_init, with_ctx)
        ob, obc = latent_attention(pb, pbc, cos, sin, b_qa_norm[l], b_w_uq[l], b_kva_norm[l], b_w_ukv[l], with_ctx)
        oc, occ = window_attention(pc, pcc, cos, sin, c_sink[l], with_ctx)
        od, odc = neighbourhood_attention(pd, pdc, d_rpb[l], with_ctx)
        x = x + g1 * gated_merge(h_lat, (oa, ob, oc, od), w_branch[l], w_gate[l], b_gate[l], w_out[l])

        x = x + g2 * peer(modulate(rms_norm(x, norm_ffn[l]), sh2, sc2), peer_wq[l], peer_keys[l], peer_u[l], peer_v[l])

        if with_ctx:
            xc = xc + g1c * gated_merge(h_ctx, (oac, obc, occ, odc), w_branch[l], w_gate[l], b_gate[l], w_out[l])
            xc = xc + g2c * peer(modulate(rms_norm(xc, norm_ffn[l]), sh2c, sc2c), peer_wq[l], peer_keys[l], peer_u[l], peer_v[l])
    return rms_norm(x, final_norm)
```

```python
import functools
import math

import numpy as np
import jax
import jax.numpy as jnp
from jax import lax
from jax.experimental import pallas as pl
from jax.experimental.pallas import tpu as pltpu

F32 = jnp.float32
BF16 = jnp.bfloat16

GRID_W = 64
ROPE_THETA = 10000.0
ROPE_DIM = 64
EPS = 1e-6
NEG = -1e30
LOG2E = 1.4426950408889634

A_HEADS, A_QK, A_V = 4, 64, 128
B_HEADS, B_QL, B_KVL, B_NOPE, B_ROPE, B_V = 4, 512, 256, 128, 64, 128
C_HEADS, C_KVH, C_GROUP, C_HD, C_WIN = 8, 2, 4, 64, 128
D_HEADS, D_HD, NA_ROWS, NA_COLS = 4, 128, 8, 16
N_BR, BR_W = 4, 512
P_HEADS, P_KEYS, P_KD, P_TOPK = 8, 128, 256, 16

LANE = 128
VMEM_LIMIT = 56 * 1024 * 1024

COL_A_Q, COL_A_K, COL_A_V = 0, 512, 1024
COL_B_QL, COL_B_KVL, COL_B_KPE = 1536, 2048, 2304
COL_C_Q, COL_C_K, COL_C_V = 2560, 3072, 3200
COL_D_Q, COL_D_K, COL_D_V = 3328, 3840, 4352
IN_PAD = 5120


def _tile(n, target, mult=LANE):
    best = None
    for t in range(mult, min(n, target) + 1, mult):
        if n % t == 0:
            best = t
    assert best is not None, (n, target)
    return best


def _params(sem):
    return pltpu.CompilerParams(dimension_semantics=sem, vmem_limit_bytes=VMEM_LIMIT)


def _dot_nt(a, b):
    return lax.dot_general(a, b, (((1,), (1,)), ((), ())), preferred_element_type=F32)


def _row_select(i, tm, n_lat, lat_row, ctx_row):
    rows = i * tm + lax.broadcasted_iota(jnp.int32, (tm, 1), 0)
    return jnp.where(rows >= n_lat, ctx_row, lat_row)


def _swap_halves64(x):
    outs = []
    lane = lax.broadcasted_iota(jnp.int32, (x.shape[0], LANE), 1) & 63
    for c in range(x.shape[1] // LANE):
        xc = x[:, c * LANE:(c + 1) * LANE]
        outs.append(jnp.where(lane < 32, pltpu.roll(xc, LANE - 32, 1), pltpu.roll(xc, 32, 1)))
    return outs[0] if len(outs) == 1 else jnp.concatenate(outs, axis=1)


def _ada_body(c_ref, w_ref, b_ref, o_ref):
    c = c_ref[...]
    s = c * (1.0 / (1.0 + jnp.exp(-c)))
    o_ref[0] = jnp.dot(s.astype(BF16), w_ref[0].astype(BF16), preferred_element_type=F32) + b_ref[0]


def _ada(c8, ada_w, ada_b):
    depth, d, n = ada_w.shape
    tn = _tile(n, 1024)
    return pl.pallas_call(
        _ada_body,
        grid=(depth, n // tn),
        in_specs=[pl.BlockSpec((8, d), lambda l, j: (0, 0)),
                  pl.BlockSpec((1, d, tn), lambda l, j: (l, 0, j)),
                  pl.BlockSpec((1, 1, tn), lambda l, j: (l, 0, j))],
        out_specs=pl.BlockSpec((1, 8, tn), lambda l, j: (l, 0, j)),
        out_shape=jax.ShapeDtypeStruct((depth, 8, n), F32),
        compiler_params=_params(("arbitrary", "arbitrary")),
        name="ada",
    )(c8, ada_w, ada_b.reshape(depth, 1, n))


def _nmm_body(*refs, n_lat, tm, has_mod, rope, want_ht):
    it = iter(refs)
    x_ref, g_ref = next(it), next(it)
    mod_ref = next(it) if has_mod else None
    w_ref = next(it)
    if rope:
        cos_ref, sin_ref, flag_ref = next(it), next(it), next(it)
    o_ref = next(it)
    ht_ref = next(it) if want_ht else None
    h_scr = next(it)
    i, j = pl.program_id(0), pl.program_id(1)

    @pl.when(j == 0)
    def _():
        x = x_ref[...].astype(F32)
        y = x * lax.rsqrt(jnp.mean(x * x, axis=-1, keepdims=True) + EPS) * g_ref[...]
        if has_mod:
            shift = _row_select(i, tm, n_lat, mod_ref[0:1, :], mod_ref[1:2, :])
            scale = _row_select(i, tm, n_lat, mod_ref[2:3, :], mod_ref[3:4, :])
            y = y * (1.0 + scale) + shift
        h_scr[...] = y.astype(BF16)
        if want_ht:
            ht_ref[...] = y.T.astype(BF16)

    acc = jnp.dot(h_scr[...], w_ref[...], preferred_element_type=F32)
    if rope:
        reps = acc.shape[1] // LANE
        flag = flag_ref[...]
        cos = 1.0 + flag * (jnp.tile(cos_ref[...], (1, reps)) - 1.0)
        sin = flag * jnp.tile(sin_ref[...], (1, reps))
        acc = acc * cos + _swap_halves64(acc) * sin
    o_ref[...] = acc.astype(o_ref.dtype)


def _nmm(x, xcol, kc, g, mod, w, *, n_lat, tm, tn, out_dtype, rope=None, want_ht=False):
    m = x.shape[0]
    n = w.shape[1]
    in_specs = [pl.BlockSpec((tm, kc), lambda i, j: (i, xcol)),
                pl.BlockSpec((1, kc), lambda i, j: (0, 0))]
    args = [x, g.reshape(1, kc)]
    if mod is not None:
        in_specs.append(pl.BlockSpec((4, kc), lambda i, j: (0, 0)))
        args.append(mod)
    in_specs.append(pl.BlockSpec((kc, tn), lambda i, j: (0, j)))
    args.append(w)
    if rope is not None:
        cos, sin, flags = rope
        in_specs += [pl.BlockSpec((tm, LANE), lambda i, j: (i, 0)),
                     pl.BlockSpec((tm, LANE), lambda i, j: (i, 0)),
                     pl.BlockSpec((1, tn), lambda i, j: (0, j))]
        args += [cos, sin, flags]
    out_specs = [pl.BlockSpec((tm, tn), lambda i, j: (i, j))]
    out_shape = [jax.ShapeDtypeStruct((m, n), out_dtype)]
    if want_ht:
        out_specs.append(pl.BlockSpec((kc, tm), lambda i, j: (0, i)))
        out_shape.append(jax.ShapeDtypeStruct((kc, m), BF16))
    body = functools.partial(_nmm_body, n_lat=n_lat, tm=tm, has_mod=mod is not None,
                             rope=rope is not None, want_ht=want_ht)
    res = pl.pallas_call(
        body, grid=(m // tm, n // tn), in_specs=in_specs, out_specs=out_specs, out_shape=out_shape,
        scratch_shapes=[pltpu.VMEM((tm, kc), BF16)],
        compiler_params=_params(("arbitrary", "arbitrary")),
        name="norm_matmul",
    )(*args)
    return res if want_ht else res[0]


def _flash_body(*refs, mode, tq, nk, tk, scale2, lam_init):
    it = iter(refs)
    q_ref = next(it)
    if mode == "cat":
        kn_ref, kp_ref = next(it), next(it)
    else:
        k_ref = next(it)
    v_ref = next(it)
    if mode == "diff":
        lam_ref, subg_ref = next(it), next(it)
    o_ref = next(it)
    if mode == "cat":
        kcat = next(it)
    m_scr, l_scr, acc_scr = next(it), next(it), next(it)

    q = q_ref[...]
    if mode == "diff":
        lane = lax.broadcasted_iota(jnp.int32, q.shape, 1)
        zero = jnp.zeros_like(q)
        q = jnp.concatenate([jnp.where(lane < A_QK, q, zero), jnp.where(lane >= A_QK, q, zero)], axis=0)
    if mode == "cat":
        @pl.when(pl.program_id(1) == 0)
        def _():
            kcat[:, :LANE] = kn_ref[...]
            kcat[:, LANE:] = kp_ref[...]
        k_src = kcat
    else:
        k_src = k_ref

    m_scr[...] = jnp.full(m_scr.shape, NEG, F32)
    l_scr[...] = jnp.zeros(l_scr.shape, F32)
    acc_scr[...] = jnp.zeros(acc_scr.shape, F32)

    def chunk(c, carry):
        off = pl.multiple_of(c * tk, tk)
        k = k_src[pl.ds(off, tk), :]
        v = v_ref[pl.ds(off, tk), :]
        s = _dot_nt(q, k) * scale2
        m_prev = m_scr[...]
        m_new = jnp.maximum(m_prev, jnp.max(s, axis=1, keepdims=True))
        alpha = jnp.exp2(m_prev - m_new)
        p = jnp.exp2(s - m_new)
        l_scr[...] = alpha * l_scr[...] + jnp.sum(p, axis=1, keepdims=True)
        acc_scr[...] = alpha * acc_scr[...] + jnp.dot(p.astype(BF16), v, preferred_element_type=F32)
        m_scr[...] = m_new
        return carry

    lax.fori_loop(0, nk // tk, chunk, 0)
    o = acc_scr[...] / l_scr[...]
    if mode == "diff":
        lp = lam_ref[...]
        lam = (jnp.exp(jnp.sum(lp[0:1] * lp[1:2], axis=1, keepdims=True))
               - jnp.exp(jnp.sum(lp[2:3] * lp[3:4], axis=1, keepdims=True)) + lam_init)
        d = o[:tq] - lam * o[tq:]
        o = d * lax.rsqrt(jnp.mean(d * d, axis=-1, keepdims=True) + EPS) * subg_ref[...] * (1.0 - lam_init)
    o_ref[...] = o.astype(o_ref.dtype)


def _flash(mode, q_arr, q_col0, dq, k_arrs, v_arr, v_col0, dv, *, heads, q_row0, n_q, kv_row0, nk,
           tq, tk, scale, out_cols, extra=(), lam_init=0.0):
    qb0 = q_row0 // tq
    kb0 = kv_row0 // nk
    qc0 = q_col0 // dq
    in_specs = [pl.BlockSpec((tq, dq), lambda h, i: (qb0 + i, qc0 + h))]
    args = [q_arr]
    for arr, col0, per_head in k_arrs:
        kc0 = col0 // LANE
        in_specs.append(pl.BlockSpec((nk, LANE), lambda h, i, kc0=kc0, ph=per_head: (kb0, kc0 + h * ph)))
        args.append(arr)
    vc0 = v_col0 // dv
    in_specs.append(pl.BlockSpec((nk, dv), lambda h, i: (kb0, vc0 + h)))
    args.append(v_arr)
    for e in extra:
        in_specs.append(pl.BlockSpec(e.shape, lambda h, i: (0, 0)))
        args.append(e)
    rows = 2 * tq if mode == "diff" else tq
    scratch = []
    if mode == "cat":
        scratch.append(pltpu.VMEM((nk, 2 * LANE), BF16))
    scratch += [pltpu.VMEM((rows, 1), F32), pltpu.VMEM((rows, 1), F32), pltpu.VMEM((rows, dv), F32)]
    body = functools.partial(_flash_body, mode=mode, tq=tq, nk=nk, tk=tk, scale2=scale * LOG2E,
                             lam_init=lam_init)
    return pl.pallas_call(
        body, grid=(heads, n_q // tq), in_specs=in_specs,
        out_specs=pl.BlockSpec((tq, dv), lambda h, i: (i, h)),
        out_shape=jax.ShapeDtypeStruct((n_q, out_cols), BF16),
        scratch_shapes=scratch,
        compiler_params=_params(("arbitrary", "arbitrary")),
        name="flash_" + mode,
    )(*args)


def _cwin_body(sink_ref, q_ref, k_ref, v_ref, o_ref, *, t, n_lat, n_ctx, with_band, scale2):
    i = pl.program_id(0)
    qf = q_ref[...].astype(F32)
    lane = lax.broadcasted_iota(jnp.int32, (t, LANE), 1)
    kc = k_ref[n_lat:n_lat + n_ctx, :]
    vc = v_ref[n_lat:n_lat + n_ctx, :]
    if with_band:
        nb = t + 2 * C_WIN
        q0 = i * t
        kstart = pl.multiple_of(jnp.clip(q0 - C_WIN, 0, n_lat - nb), LANE)
        kb = k_ref[pl.ds(kstart, nb), :]
        vb = v_ref[pl.ds(kstart, nb), :]
        qpos = q0 + (lax.broadcasted_iota(jnp.int32, (C_GROUP * t, nb), 0) & (t - 1))
        kpos = kstart + lax.broadcasted_iota(jnp.int32, (C_GROUP * t, nb), 1)
        valid = jnp.abs(qpos - kpos) <= C_WIN
    grp = lax.shift_right_logical(lax.broadcasted_iota(jnp.int32, (C_GROUP * t, 1), 0), int(math.log2(t)))
    outs = [None] * C_HEADS
    for j in range(C_KVH):
        in_head = (lane >= j * C_HD) & (lane < (j + 1) * C_HD)
        qs = []
        for g in range(C_GROUP):
            hq = j * C_GROUP + g
            chunk = qf[:, (hq // 2) * LANE:(hq // 2 + 1) * LANE]
            if hq % 2 != j:
                chunk = pltpu.roll(chunk, C_HD, 1)
            qs.append(jnp.where(in_head, chunk, 0.0).astype(BF16))
        qj = jnp.concatenate(qs, axis=0)
        sink = jnp.zeros((C_GROUP * t, 1), F32)
        for g in range(C_GROUP):
            sink = jnp.where(grp == g, sink_ref[j * C_GROUP + g] * LOG2E, sink)
        sc = _dot_nt(qj, kc) * scale2
        m = jnp.maximum(jnp.max(sc, axis=1, keepdims=True), sink)
        if with_band:
            sb = jnp.where(valid, _dot_nt(qj, kb) * scale2, NEG)
            m = jnp.maximum(m, jnp.max(sb, axis=1, keepdims=True))
            pb = jnp.exp2(sb - m)
        pc = jnp.exp2(sc - m)
        l = jnp.sum(pc, axis=1, keepdims=True) + jnp.exp2(sink - m)
        o = jnp.dot(pc.astype(BF16), vc, preferred_element_type=F32)
        if with_band:
            l = l + jnp.sum(pb, axis=1, keepdims=True)
            o = o + jnp.dot(pb.astype(BF16), vb, preferred_element_type=F32)
        o = o / l
        for g in range(C_GROUP):
            og = o[g * t:(g + 1) * t]
            if g % 2 != j:
                og = pltpu.roll(og, C_HD, 1)
            outs[j * C_GROUP + g] = og
    cols = [jnp.where(lane < C_HD, outs[2 * c], outs[2 * c + 1]) for c in range(C_HEADS // 2)]
    o_ref[...] = jnp.concatenate(cols, axis=1).astype(o_ref.dtype)


def _cwin(p, sink, *, n_lat, n_ctx, q_row0, n_q, with_band):
    ntok = p.shape[0]
    t = 256 if with_band else n_q
    qb0 = q_row0 // t
    qw = C_HEADS * C_HD
    body = functools.partial(_cwin_body, t=t, n_lat=n_lat, n_ctx=n_ctx, with_band=with_band,
                             scale2=C_HD ** -0.5 * LOG2E)
    return pl.pallas_call(
        body, grid=(n_q // t,),
        in_specs=[pl.BlockSpec(memory_space=pltpu.SMEM),
                  pl.BlockSpec((t, qw), lambda i: (qb0 + i, COL_C_Q // qw)),
                  pl.BlockSpec((ntok, LANE), lambda i: (0, COL_C_K // LANE)),
                  pl.BlockSpec((ntok, LANE), lambda i: (0, COL_C_V // LANE))],
        out_specs=pl.BlockSpec((t, qw), lambda i: (i, 0)),
        out_shape=jax.ShapeDtypeStruct((n_q, qw), BF16),
        compiler_params=_params(("arbitrary",)),
        name="window_attn",
    )(sink, p, p, p)


D_QROWS = 8
D_KROWS = 16


def _dna_window_row(i, rows):
    return jnp.clip(D_QROWS * i - NA_ROWS // 2, 0, rows - D_KROWS)


def _dna_body(q_ref, k_ref, v_ref, bias_ref, o_ref, *, n_lat, n_ctx, rows, scale):
    i = pl.program_id(1)
    nkw = D_KROWS * GRID_W
    kstart = pl.multiple_of(_dna_window_row(i, rows) * GRID_W, LANE)
    q = q_ref[...]
    kw = k_ref[pl.ds(kstart, nkw), :]
    vw = v_ref[pl.ds(kstart, nkw), :]
    kc = k_ref[n_lat:n_lat + n_ctx, :]
    vc = v_ref[n_lat:n_lat + n_ctx, :]
    sw = _dot_nt(q, kw) * scale + bias_ref[0, 0]
    sc = _dot_nt(q, kc) * scale
    m = jnp.maximum(jnp.max(sw, axis=1, keepdims=True), jnp.max(sc, axis=1, keepdims=True))
    pw = jnp.exp(sw - m)
    pc = jnp.exp(sc - m)
    l = jnp.sum(pw, axis=1, keepdims=True) + jnp.sum(pc, axis=1, keepdims=True)
    o = (jnp.dot(pw.astype(BF16), vw, preferred_element_type=F32)
         + jnp.dot(pc.astype(BF16), vc, preferred_element_type=F32))
    o_ref[...] = (o / l).astype(o_ref.dtype)


def _dna_bias_table(rpb, rows):
    nq, nkw = D_QROWS * GRID_W, D_KROWS * GRID_W
    ri, c = np.divmod(np.arange(nq), GRID_W)
    kr, kcol = np.divmod(np.arange(nkw), GRID_W)
    cs = np.clip(c - NA_COLS // 2, 0, GRID_W - NA_COLS)
    col_ok = (kcol[None, :] >= cs[:, None]) & (kcol[None, :] < cs[:, None] + NA_COLS)
    col_idx = np.clip(kcol[None, :] - c[:, None] + NA_COLS - 1, 0, 2 * NA_COLS - 2)
    tabs = []
    for d, rel in ((0, np.maximum(ri - 4, 0)), (4, ri), (8, np.minimum(ri + 4, 8))):
        row_ok = (kr[None, :] >= rel[:, None]) & (kr[None, :] < rel[:, None] + NA_ROWS)
        row_idx = np.clip(kr[None, :] - d - ri[:, None] + NA_ROWS - 1, 0, 2 * NA_ROWS - 2)
        ok = jnp.asarray(row_ok & col_ok)
        tabs.append(jnp.where(ok[None], rpb[:, row_idx, col_idx], NEG))
    return jnp.stack(tabs, axis=1)


def _dna(p, bias_tbl, *, n_lat, n_ctx):
    ntok = p.shape[0]
    rows = n_lat // GRID_W
    nq, nkw = D_QROWS * GRID_W, D_KROWS * GRID_W

    def variant(i):
        return (D_QROWS * i - _dna_window_row(i, rows)) // 4

    body = functools.partial(_dna_body, n_lat=n_lat, n_ctx=n_ctx, rows=rows, scale=D_HD ** -0.5)
    return pl.pallas_call(
        body, grid=(D_HEADS, rows // D_QROWS),
        in_specs=[pl.BlockSpec((nq, D_HD), lambda h, i: (i, COL_D_Q // D_HD + h)),
                  pl.BlockSpec((ntok, D_HD), lambda h, i: (0, COL_D_K // D_HD + h)),
                  pl.BlockSpec((ntok, D_HD), lambda h, i: (0, COL_D_V // D_HD + h)),
                  pl.BlockSpec((1, 1, nq, nkw), lambda h, i: (h, variant(i), 0, 0))],
        out_specs=pl.BlockSpec((nq, D_HD), lambda h, i: (i, h)),
        out_shape=jax.ShapeDtypeStruct((n_lat, D_HEADS * D_HD), BF16),
        compiler_params=_params(("arbitrary", "arbitrary")),
        name="neighbourhood_attn",
    )(p, p, p, bias_tbl)


def _merge_body(x_ref, g_ref, mod_ref, oa_ref, ob_ref, oc_ref, od_ref, wg_ref, bg_ref, wb_ref, o_ref, h_scr,
                *, n_lat, tm):
    i, j = pl.program_id(0), pl.program_id(1)

    @pl.when(j == 0)
    def _():
        x = x_ref[...]
        y = x * lax.rsqrt(jnp.mean(x * x, axis=-1, keepdims=True) + EPS) * g_ref[...]
        shift = _row_select(i, tm, n_lat, mod_ref[0:1, :], mod_ref[1:2, :])
        scale = _row_select(i, tm, n_lat, mod_ref[2:3, :], mod_ref[3:4, :])
        h_scr[...] = (y * (1.0 + scale) + shift).astype(BF16)

    h = h_scr[...]
    merged = None
    for b, o_b in enumerate((oa_ref, ob_ref, oc_ref, od_ref)):
        z = jnp.dot(h, wg_ref[b], preferred_element_type=F32) + bg_ref[b]
        gate = 1.0 / (1.0 + jnp.exp(-z))
        term = gate * jnp.dot(o_b[...], wb_ref[b], preferred_element_type=F32)
        merged = term if merged is None else merged + term
    o_ref[...] = merged.astype(o_ref.dtype)


def _merge(x, g, mod, branches, wg, bg, wb, *, n_lat, tm, tn):
    m, d = x.shape
    body = functools.partial(_merge_body, n_lat=n_lat, tm=tm)
    bspec = pl.BlockSpec((tm, BR_W), lambda i, j: (i, 0))
    return pl.pallas_call(
        body, grid=(m // tm, d // tn),
        in_specs=[pl.BlockSpec((tm, d), lambda i, j: (i, 0)),
                  pl.BlockSpec((1, d), lambda i, j: (0, 0)),
                  pl.BlockSpec((4, d), lambda i, j: (0, 0)),
                  bspec, bspec, bspec, bspec,
                  pl.BlockSpec((N_BR, d, tn), lambda i, j: (0, 0, j)),
                  pl.BlockSpec((N_BR, 1, tn), lambda i, j: (0, 0, j)),
                  pl.BlockSpec((N_BR, BR_W, tn), lambda i, j: (0, 0, j))],
        out_specs=pl.BlockSpec((tm, tn), lambda i, j: (i, j)),
        out_shape=jax.ShapeDtypeStruct((m, d), BF16),
        scratch_shapes=[pltpu.VMEM((tm, d), BF16)],
        compiler_params=_params(("arbitrary", "arbitrary")),
        name="gated_merge",
    )(x, g.reshape(1, d), mod, *branches, wg, bg, wb)


def _resmm_body(a_ref, w_ref, res_ref, gate_ref, o_ref, *, n_lat, tm):
    i = pl.program_id(0)
    gate = _row_select(i, tm, n_lat, gate_ref[0:1, :], gate_ref[1:2, :])
    o_ref[...] = res_ref[...] + gate * jnp.dot(a_ref[...], w_ref[...], preferred_element_type=F32)


def _resmm(a, w, res, gate, *, n_lat, tm, tn):
    m, k = a.shape
    n = w.shape[1]
    body = functools.partial(_resmm_body, n_lat=n_lat, tm=tm)
    return pl.pallas_call(
        body, grid=(m // tm, n // tn),
        in_specs=[pl.BlockSpec((tm, k), lambda i, j: (i, 0)),
                  pl.BlockSpec((k, tn), lambda i, j: (0, j)),
                  pl.BlockSpec((tm, tn), lambda i, j: (i, j)),
                  pl.BlockSpec((2, tn), lambda i, j: (0, j))],
        out_specs=pl.BlockSpec((tm, tn), lambda i, j: (i, j)),
        out_shape=jax.ShapeDtypeStruct((m, n), F32),
        compiler_params=_params(("arbitrary", "arbitrary")),
        name="residual_matmul",
    )(a, w, res, gate)


P_RANKS = P_TOPK + 1
P_PAIRS = tuple((a, b) for a in range(P_RANKS) for b in range(P_RANKS) if (a + 1) * (b + 1) <= P_RANKS)


def _peer_topk_body(q_ref, knh_ref, khn_ref, ae_ref, be_ref, t_ref, cur_scr, top_scr, cand_scr, sum_scr, *, tt):
    q = q_ref[...]
    hk = P_HEADS * P_KEYS
    s_hn = []
    for p in range(2):
        qp = q[:, p * hk:(p + 1) * hk]
        cur_scr[...] = _dot_nt(knh_ref[p], qp).reshape(P_KEYS, P_HEADS, tt)
        s_hn.append(_dot_nt(khn_ref[p], qp).reshape(P_HEADS, P_KEYS, tt))

        def rank(r, carry, p=p):
            c = cur_scr[...]
            mx = jnp.max(c, axis=0)
            cur_scr[...] = jnp.where(c == mx[None], NEG, c)
            top_scr[p * P_RANKS + r] = mx
            return carry

        lax.fori_loop(0, P_RANKS, rank, 0)

    for n, (a, b) in enumerate(P_PAIRS):
        cand_scr[n] = top_scr[a] + top_scr[P_RANKS + b]

    def rank_sum(r, carry):
        c = cand_scr[...]
        mx = jnp.max(c, axis=0)
        cand_scr[...] = jnp.where(c == mx[None], NEG, c)
        sum_scr[r] = mx
        return carry

    lax.fori_loop(0, P_RANKS, rank_sum, 0)

    t0 = sum_scr[0]
    z = jnp.zeros_like(t0)
    for r in range(P_TOPK):
        z = z + jnp.exp(sum_scr[r] - t0)
    inv_z = 1.0 / z
    t_ref[...] = jnp.exp(0.5 * (sum_scr[P_TOPK - 1] + sum_scr[P_TOPK]) - t0) * inv_z
    mx0, mx1 = top_scr[0], top_scr[P_RANKS]
    ae_ref[...] = (jnp.exp(s_hn[0] - mx0[:, None, :]) * inv_z[:, None, :]).reshape(hk, tt)
    be_ref[...] = jnp.exp(s_hn[1] - mx1[:, None, :]).reshape(hk, tt)


def _peer_topk(q, knh, khn, *, tt):
    m = q.shape[0]
    hk = P_HEADS * P_KEYS
    body = functools.partial(_peer_topk_body, tt=tt)
    return pl.pallas_call(
        body, grid=(m // tt,),
        in_specs=[pl.BlockSpec((tt, 2 * hk), lambda i: (i, 0)),
                  pl.BlockSpec((2, hk, hk), lambda i: (0, 0, 0)),
                  pl.BlockSpec((2, hk, hk), lambda i: (0, 0, 0))],
        out_specs=[pl.BlockSpec((hk, tt), lambda i: (0, i)),
                   pl.BlockSpec((hk, tt), lambda i: (0, i)),
                   pl.BlockSpec((P_HEADS, tt), lambda i: (0, i))],
        out_shape=[jax.ShapeDtypeStruct((hk, m), F32), jax.ShapeDtypeStruct((hk, m), F32),
                   jax.ShapeDtypeStruct((P_HEADS, m), F32)],
        scratch_shapes=[pltpu.VMEM((P_KEYS, P_HEADS, tt), F32),
                        pltpu.VMEM((2 * P_RANKS, P_HEADS, tt), F32),
                        pltpu.VMEM((len(P_PAIRS), P_HEADS, tt), F32),
                        pltpu.VMEM((P_RANKS, P_HEADS, tt), F32)],
        compiler_params=_params(("arbitrary",)),
        name="peer_topk",
    )(q, knh, khn)


def _peer_dense_body(u_ref, ht_ref, vt_ref, ae_ref, be_ref, t_ref, acc_ref, p_scr, *, te):
    e = pl.program_id(1)

    @pl.when(e == 0)
    def _():
        acc_ref[...] = jnp.zeros(acc_ref.shape, F32)

    act = jnp.dot(u_ref[...], ht_ref[...], preferred_element_type=F32)
    act = 0.5 * act * (1.0 + lax.erf(act * math.sqrt(0.5)))
    n_i = te // P_KEYS
    for ii in range(n_i):
        i = e * n_i + ii
        w = None
        for h in range(P_HEADS):
            a_row = ae_ref[pl.ds(h * P_KEYS + i, 1), :]
            prod = a_row * be_ref[h * P_KEYS:(h + 1) * P_KEYS, :]
            term = jnp.where(prod >= t_ref[h:h + 1, :], prod, 0.0)
            w = term if w is None else w + term
        p_scr[ii * P_KEYS:(ii + 1) * P_KEYS, :] = (w * act[ii * P_KEYS:(ii + 1) * P_KEYS, :]).astype(BF16)
    acc_ref[...] += jnp.dot(vt_ref[...], p_scr[...], preferred_element_type=F32)


def _peer_dense(u, ht, vt, ae, be, thr, *, tm, te):
    n_exp, d = u.shape
    m = ht.shape[1]
    hk = P_HEADS * P_KEYS
    body = functools.partial(_peer_dense_body, te=te)
    return pl.pallas_call(
        body, grid=(m // tm, n_exp // te),
        in_specs=[pl.BlockSpec((te, d), lambda t, e: (e, 0)),
                  pl.BlockSpec((d, tm), lambda t, e: (0, t)),
                  pl.BlockSpec((d, te), lambda t, e: (0, e)),
                  pl.BlockSpec((hk, tm), lambda t, e: (0, t)),
                  pl.BlockSpec((hk, tm), lambda t, e: (0, t)),
                  pl.BlockSpec((P_HEADS, tm), lambda t, e: (0, t))],
        out_specs=pl.BlockSpec((d, tm), lambda t, e: (0, t)),
        out_shape=jax.ShapeDtypeStruct((d, m), F32),
        scratch_shapes=[pltpu.VMEM((te, tm), BF16)],
        compiler_params=_params(("arbitrary", "arbitrary")),
        name="peer_dense",
    )(u, ht, vt, ae, be, thr)


def _peer_out_body(x_ref, at_ref, gate_ref, o_ref, *, n_lat, tm):
    gate = _row_select(pl.program_id(0), tm, n_lat, gate_ref[0:1, :], gate_ref[1:2, :])
    o_ref[...] = x_ref[...] + gate * at_ref[...].T


def _peer_out(x, acc_t, gate, *, n_lat, tm):
    m, d = x.shape
    body = functools.partial(_peer_out_body, n_lat=n_lat, tm=tm)
    return pl.pallas_call(
        body, grid=(m // tm,),
        in_specs=[pl.BlockSpec((tm, d), lambda i: (i, 0)),
                  pl.BlockSpec((d, tm), lambda i: (0, i)),
                  pl.BlockSpec((2, d), lambda i: (0, 0))],
        out_specs=pl.BlockSpec((tm, d), lambda i: (i, 0)),
        out_shape=jax.ShapeDtypeStruct((m, d), F32),
        compiler_params=_params(("arbitrary",)),
        name="peer_residual",
    )(x, acc_t, gate)


def _final_norm_body(x_ref, g_ref, o_ref):
    x = x_ref[...]
    o_ref[...] = x * lax.rsqrt(jnp.mean(x * x, axis=-1, keepdims=True) + EPS) * g_ref[...]


def _final_norm(x, g, *, n_rows, tm):
    d = x.shape[1]
    return pl.pallas_call(
        _final_norm_body, grid=(n_rows // tm,),
        in_specs=[pl.BlockSpec((tm, d), lambda i: (i, 0)), pl.BlockSpec((1, d), lambda i: (0, 0))],
        out_specs=pl.BlockSpec((tm, d), lambda i: (i, 0)),
        out_shape=jax.ShapeDtypeStruct((n_rows, d), F32),
        compiler_params=_params(("arbitrary",)),
        name="final_norm",
    )(x, g.reshape(1, d))


def _rope_tables(n_lat, n_ctx):
    t = jnp.arange(n_lat, dtype=jnp.int32)
    row = (t // GRID_W).astype(F32)
    col = (t % GRID_W).astype(F32)
    n_freq = ROPE_DIM // 4
    inv = ROPE_THETA ** (-jnp.arange(n_freq, dtype=F32) / n_freq)
    ang = jnp.concatenate([row[:, None] * inv, col[:, None] * inv], axis=-1)
    cos, sin = jnp.cos(ang), jnp.sin(ang)
    cos128 = jnp.concatenate([jnp.tile(cos, (1, 4)), jnp.ones((n_ctx, LANE), F32)], axis=0)
    sin128 = jnp.concatenate([jnp.tile(jnp.concatenate([-sin, sin], axis=1), (1, 2)),
                              jnp.zeros((n_ctx, LANE), F32)], axis=0)
    return cos128, sin128


def _rope_flags(width, ranges):
    f = np.zeros((1, width), np.float32)
    for lo, hi in ranges:
        f[0, lo:hi] = 1.0
    return jnp.asarray(f)


def _pad_heads(w, heads, width, padded):
    k = w.shape[0]
    w = w.reshape(k, heads, width)
    return jnp.pad(w, ((0, 0), (0, 0), (0, padded - width))).reshape(k, heads * padded)


def kernel(x, c, ctx, c_ctx, ada_w, ada_b, norm_mix, norm_ffn, w_in, a_lam_q1, a_lam_k1, a_lam_q2, a_lam_k2,
           a_subln, b_qa_norm, b_w_uq, b_kva_norm, b_w_ukv, c_sink, d_rpb, w_branch, w_gate, b_gate, w_out,
           peer_wq, peer_keys, peer_u, peer_v, final_norm):
    assert x.shape[0] == 1 and ctx.shape[0] == 1
    n_lat, d = x.shape[1], x.shape[2]
    n_ctx = ctx.shape[1]
    depth = ada_w.shape[0]
    ntok = n_lat + n_ctx
    rows = n_lat // GRID_W
    assert n_lat % n_ctx == 0 and n_lat % (D_QROWS * GRID_W) == 0 and rows >= D_KROWS

    tm = _tile(ntok, 768)
    xs = jnp.concatenate([x[0], ctx[0]], axis=0)
    cos128, sin128 = _rope_tables(n_lat, n_ctx)
    in_flags = _rope_flags(IN_PAD, [(COL_A_Q, COL_A_V), (COL_B_KPE, COL_B_KPE + B_ROPE), (COL_C_Q, COL_C_V)])
    uq_flags = _rope_flags(B_HEADS * 2 * LANE, [(h * 2 * LANE + B_NOPE, h * 2 * LANE + B_NOPE + B_ROPE)
                                                for h in range(B_HEADS)])

    c8 = jnp.zeros((8, d), F32).at[0].set(c[0]).at[1].set(c_ctx)
    mods = _ada(c8, ada_w, ada_b)

    eye = jnp.eye(P_HEADS, dtype=F32)
    for l in range(depth):
        lam_init = 0.8 - 0.6 * math.exp(-0.3 * l)
        mod6 = mods[l, :2].reshape(2, 6, d)
        mod1 = jnp.concatenate([mod6[:, 0], mod6[:, 1]], axis=0)
        mod2 = jnp.concatenate([mod6[:, 3], mod6[:, 4]], axis=0)
        gate1, gate2 = mod6[:, 2], mod6[:, 5]

        w = w_in[l]
        k_pe_end = COL_B_KPE + B_ROPE
        w_in_p = jnp.concatenate([w[:, :k_pe_end], jnp.zeros((d, COL_C_Q - k_pe_end), F32), w[:, k_pe_end:],
                                  jnp.zeros((d, IN_PAD - (COL_D_V + D_HEADS * D_HD)), F32)], axis=1).astype(BF16)
        w_uq_p = _pad_heads(b_w_uq[l], B_HEADS, B_NOPE + B_ROPE, 2 * LANE).astype(BF16)
        w_ukv = b_w_ukv[l].reshape(B_KVL, B_HEADS, B_NOPE + B_V)
        w_ukv_p = jnp.concatenate([w_ukv[:, :, :B_NOPE].reshape(B_KVL, -1), w_ukv[:, :, B_NOPE:].reshape(B_KVL, -1)],
                                  axis=1).astype(BF16)
        wq_p = peer_wq[l].reshape(d, P_HEADS, 2, P_KD // 2).transpose(0, 2, 1, 3).reshape(d, -1).astype(BF16)
        keys = peer_keys[l]
        knh = jnp.stack([jnp.einsum('hnd,hg->nhgd', keys[:, p], eye).reshape(P_HEADS * P_KEYS, -1)
                         for p in range(2)]).astype(BF16)
        khn = jnp.stack([jnp.einsum('hnd,hg->hngd', keys[:, p], eye).reshape(P_HEADS * P_KEYS, -1)
                         for p in range(2)]).astype(BF16)
        u_b = peer_u[l].astype(BF16)
        vt_b = peer_v[l].T.astype(BF16)

        p = _nmm(xs, 0, d, norm_mix[l], mod1, w_in_p, n_lat=n_lat, tm=tm, tn=_tile(IN_PAD, 1024),
                 out_dtype=BF16, rope=(cos128, sin128, in_flags))
        qb = _nmm(p, COL_B_QL // B_QL, B_QL, b_qa_norm[l], None, w_uq_p, n_lat=n_lat, tm=tm, tn=w_uq_p.shape[1],
                  out_dtype=BF16, rope=(cos128, sin128, uq_flags))
        kvb = _nmm(p, COL_B_KVL // B_KVL, B_KVL, b_kva_norm[l], None, w_ukv_p, n_lat=n_lat, tm=tm,
                   tn=w_ukv_p.shape[1], out_dtype=BF16)
        lam_p = jnp.stack([a_lam_q1[l], a_lam_k1[l], a_lam_q2[l], a_lam_k2[l]])
        subg = a_subln[l].reshape(1, A_V)
        bias_tbl = _dna_bias_table(d_rpb[l].astype(F32), rows)

        def attend(q_row0, n_q, kv_row0, nk, tq, tk):
            oa = _flash("diff", p, COL_A_Q, LANE, [(p, COL_A_K, 1)], p, COL_A_V, A_V, heads=A_HEADS,
                        q_row0=q_row0, n_q=n_q, kv_row0=kv_row0, nk=nk, tq=tq, tk=tk, scale=A_QK ** -0.5,
                        out_cols=A_HEADS * A_V, extra=(lam_p, subg), lam_init=lam_init)
            ob = _flash("cat", qb, 0, 2 * LANE, [(kvb, 0, 1), (p, COL_B_KPE, 0)], kvb, B_HEADS * B_NOPE, B_V,
                        heads=B_HEADS, q_row0=q_row0, n_q=n_q, kv_row0=kv_row0, nk=nk, tq=tq, tk=tk,
                        scale=(B_NOPE + B_ROPE) ** -0.5, out_cols=B_HEADS * B_V)
            return oa, ob

        tk = _tile(ntok, 768)
        oa, ob = attend(0, n_lat, 0, ntok, _tile(n_lat, 512), tk)
        oc = _cwin(p, c_sink[l], n_lat=n_lat, n_ctx=n_ctx, q_row0=0, n_q=n_lat, with_band=True)
        od = _dna(p, bias_tbl, n_lat=n_lat, n_ctx=n_ctx)
        if l < depth - 1:
            oac, obc = attend(n_lat, n_ctx, n_lat, n_ctx, n_ctx, n_ctx)
            occ = _cwin(p, c_sink[l], n_lat=n_lat, n_ctx=n_ctx, q_row0=n_lat, n_q=n_ctx, with_band=False)
            odc = _flash("plain", p, COL_D_Q, D_HD, [(p, COL_D_K, 1)], p, COL_D_V, D_HD, heads=D_HEADS,
                         q_row0=n_lat, n_q=n_ctx, kv_row0=n_lat, nk=n_ctx, tq=n_ctx, tk=n_ctx,
                         scale=D_HD ** -0.5, out_cols=D_HEADS * D_HD)
            branches = [jnp.concatenate([a, b], axis=0) for a, b in ((oa, oac), (ob, obc), (oc, occ), (od, odc))]
        else:
            branches = [jnp.pad(a, ((0, n_ctx), (0, 0))) for a in (oa, ob, oc, od)]
        merged = _merge(xs, norm_mix[l], mod1, branches, w_gate[l].astype(BF16), b_gate[l].reshape(N_BR, 1, d),
                        w_branch[l].astype(BF16), n_lat=n_lat, tm=tm, tn=_tile(d, 256))
        xs = _resmm(merged, w_out[l].astype(BF16), xs, gate1, n_lat=n_lat, tm=tm, tn=_tile(d, 1024))

        qp, h_t = _nmm(xs, 0, d, norm_ffn[l], mod2, wq_p, n_lat=n_lat, tm=tm, tn=_tile(d, 1024), out_dtype=BF16,
                       want_ht=True)
        ae, be, thr = _peer_topk(qp, knh, khn, tt=_tile(ntok, 256))
        acc_t = _peer_dense(u_b, h_t, vt_b, ae, be, thr, tm=tm, te=512)
        xs = _peer_out(xs, acc_t, gate2, n_lat=n_lat, tm=_tile(ntok, 256))

    out = _final_norm(xs, final_norm, n_rows=n_lat, tm=_tile(n_lat, 512))
    return out[None]
```

```python
import functools
import math

import numpy as np
import jax
import jax.numpy as jnp
from jax import lax
from jax.experimental import pallas as pl
from jax.experimental.pallas import tpu as pltpu

F32 = jnp.float32
BF16 = jnp.bfloat16

GRID_W = 64
ROPE_THETA = 10000.0
ROPE_DIM = 64
EPS = 1e-6
NEG = -1e30
LOG2E = 1.4426950408889634

A_HEADS, A_QK, A_V = 4, 64, 128
B_HEADS, B_QL, B_KVL, B_NOPE, B_ROPE, B_V = 4, 512, 256, 128, 64, 128
C_HEADS, C_KVH, C_GROUP, C_HD, C_WIN = 8, 2, 4, 64, 128
D_HEADS, D_HD, NA_ROWS, NA_COLS = 4, 128, 8, 16
N_BR, BR_W = 4, 512
P_HEADS, P_KEYS, P_KD, P_TOPK = 8, 128, 256, 16

LANE = 128
VMEM_LIMIT = 56 * 1024 * 1024

COL_A_Q, COL_A_K, COL_A_V = 0, 512, 1024
COL_B_QL, COL_B_KVL, COL_B_KPE = 1536, 2048, 2304
COL_C_Q, COL_C_K, COL_C_V = 2560, 3072, 3200
COL_D_Q, COL_D_K, COL_D_V = 3328, 3840, 4352
IN_PAD = 5120


def _tile(n, target, mult=LANE):
    best = None
    for t in range(mult, min(n, target) + 1, mult):
        if n % t == 0:
            best = t
    assert best is not None, (n, target)
    return best


def _params(sem):
    return pltpu.CompilerParams(dimension_semantics=sem, vmem_limit_bytes=VMEM_LIMIT)


def _dot_nt(a, b):
    return lax.dot_general(a, b, (((1,), (1,)), ((), ())), preferred_element_type=F32)


def _row_select(i, tm, n_lat, lat_row, ctx_row):
    rows = i * tm + lax.broadcasted_iota(jnp.int32, (tm, 1), 0)
    return jnp.where(rows >= n_lat, ctx_row, lat_row)


def _swap_halves64(x):
    outs = []
    lane = lax.broadcasted_iota(jnp.int32, (x.shape[0], LANE), 1) & 63
    for c in range(x.shape[1] // LANE):
        xc = x[:, c * LANE:(c + 1) * LANE]
        outs.append(jnp.where(lane < 32, pltpu.roll(xc, LANE - 32, 1), pltpu.roll(xc, 32, 1)))
    return outs[0] if len(outs) == 1 else jnp.concatenate(outs, axis=1)


def _ada_body(c_ref, w_ref, b_ref, o_ref):
    c = c_ref[...]
    s = c * (1.0 / (1.0 + jnp.exp(-c)))
    o_ref[0] = jnp.dot(s.astype(BF16), w_ref[0].astype(BF16), preferred_element_type=F32) + b_ref[0]


def _ada(c8, ada_w, ada_b):
    depth, d, n = ada_w.shape
    tn = _tile(n, 1024)
    return pl.pallas_call(
        _ada_body,
        grid=(depth, n // tn),
        in_specs=[pl.BlockSpec((8, d), lambda l, j: (0, 0)),
                  pl.BlockSpec((1, d, tn), lambda l, j: (l, 0, j)),
                  pl.BlockSpec((1, 1, tn), lambda l, j: (l, 0, j))],
        out_specs=pl.BlockSpec((1, 8, tn), lambda l, j: (l, 0, j)),
        out_shape=jax.ShapeDtypeStruct((depth, 8, n), F32),
        compiler_params=_params(("arbitrary", "arbitrary")),
        name="ada",
    )(c8, ada_w, ada_b.reshape(depth, 1, n))


def _nmm_body(*refs, n_lat, tm, has_mod, rope, want_ht):
    it = iter(refs)
    x_ref, g_ref = next(it), next(it)
    mod_ref = next(it) if has_mod else None
    w_ref = next(it)
    if rope:
        cos_ref, sin_ref, flag_ref = next(it), next(it), next(it)
    o_ref = next(it)
    ht_ref = next(it) if want_ht else None
    h_scr = next(it)
    i, j = pl.program_id(0), pl.program_id(1)

    @pl.when(j == 0)
    def _():
        x = x_ref[...].astype(F32)
        y = x * lax.rsqrt(jnp.mean(x * x, axis=-1, keepdims=True) + EPS) * g_ref[...]
        if has_mod:
            shift = _row_select(i, tm, n_lat, mod_ref[0:1, :], mod_ref[1:2, :])
            scale = _row_select(i, tm, n_lat, mod_ref[2:3, :], mod_ref[3:4, :])
            y = y * (1.0 + scale) + shift
        h_scr[...] = y.astype(BF16)
        if want_ht:
            ht_ref[...] = y.T.astype(BF16)

    acc = jnp.dot(h_scr[...], w_ref[...], preferred_element_type=F32)
    if rope:
        reps = acc.shape[1] // LANE
        flag = flag_ref[...]
        cos = 1.0 + flag * (jnp.tile(cos_ref[...], (1, reps)) - 1.0)
        sin = flag * jnp.tile(sin_ref[...], (1, reps))
        acc = acc * cos + _swap_halves64(acc) * sin
    o_ref[...] = acc.astype(o_ref.dtype)


def _nmm(x, xcol, kc, g, mod, w, *, n_lat, tm, tn, out_dtype, rope=None, want_ht=False):
    m = x.shape[0]
    n = w.shape[1]
    in_specs = [pl.BlockSpec((tm, kc), lambda i, j: (i, xcol)),
                pl.BlockSpec((1, kc), lambda i, j: (0, 0))]
    args = [x, g.reshape(1, kc)]
    if mod is not None:
        in_specs.append(pl.BlockSpec((4, kc), lambda i, j: (0, 0)))
        args.append(mod)
    in_specs.append(pl.BlockSpec((kc, tn), lambda i, j: (0, j)))
    args.append(w)
    if rope is not None:
        cos, sin, flags = rope
        in_specs += [pl.BlockSpec((tm, LANE), lambda i, j: (i, 0)),
                     pl.BlockSpec((tm, LANE), lambda i, j: (i, 0)),
                     pl.BlockSpec((1, tn), lambda i, j: (0, j))]
        args += [cos, sin, flags]
    out_specs = [pl.BlockSpec((tm, tn), lambda i, j: (i, j))]
    out_shape = [jax.ShapeDtypeStruct((m, n), out_dtype)]
    if want_ht:
        out_specs.append(pl.BlockSpec((kc, tm), lambda i, j: (0, i)))
        out_shape.append(jax.ShapeDtypeStruct((kc, m), BF16))
    body = functools.partial(_nmm_body, n_lat=n_lat, tm=tm, has_mod=mod is not None,
                             rope=rope is not None, want_ht=want_ht)
    res = pl.pallas_call(
        body, grid=(m // tm, n // tn), in_specs=in_specs, out_specs=out_specs, out_shape=out_shape,
        scratch_shapes=[pltpu.VMEM((tm, kc), BF16)],
        compiler_params=_params(("arbitrary", "arbitrary")),
        name="norm_matmul",
    )(*args)
    return res if want_ht else res[0]


def _flash_body(*refs, mode, tq, nk, tk, scale2, lam_init):
    it = iter(refs)
    q_ref = next(it)
    if mode == "cat":
        kn_ref, kp_ref = next(it), next(it)
    else:
        k_ref = next(it)
    v_ref = next(it)
    if mode == "diff":
        lam_ref, subg_ref = next(it), next(it)
    o_ref = next(it)
    if mode == "cat":
        kcat = next(it)
    m_scr, l_scr, acc_scr, s_scr = next(it), next(it), next(it), next(it)

    q = q_ref[...]
    if mode == "diff":
        lane = lax.broadcasted_iota(jnp.int32, q.shape, 1)
        zero = jnp.zeros_like(q)
        q = jnp.concatenate([jnp.where(lane < A_QK, q, zero), jnp.where(lane >= A_QK, q, zero)], axis=0)
    if mode == "cat":
        @pl.when(pl.program_id(1) == 0)
        def _():
            kcat[:, :LANE] = kn_ref[...]
            kcat[:, LANE:] = kp_ref[...]
        k_src = kcat
    else:
        k_src = k_ref

    m_scr[...] = jnp.full(m_scr.shape, NEG, F32)
    l_scr[...] = jnp.zeros(l_scr.shape, F32)
    acc_scr[...] = jnp.zeros(acc_scr.shape, F32)

    def scores(c):
        off = pl.multiple_of(c * tk, tk)
        s = _dot_nt(q, k_src[pl.ds(off, tk), :])
        return s if scale2 is None else s * scale2

    def update(s, c):
        off = pl.multiple_of(c * tk, tk)
        v = v_ref[pl.ds(off, tk), :]
        m_prev = m_scr[...]
        m_new = jnp.maximum(m_prev, jnp.max(s, axis=1, keepdims=True))
        alpha = jnp.exp2(m_prev - m_new)
        p = jnp.exp2(s - m_new)
        l_scr[...] = alpha * l_scr[...] + jnp.sum(p, axis=1, keepdims=True)
        acc_scr[...] = alpha * acc_scr[...] + jnp.dot(p.astype(BF16), v, preferred_element_type=F32)
        m_scr[...] = m_new

    n_chunks = nk // tk
    pairs = (n_chunks - 1) // 2
    s_scr[...] = scores(0)

    def trip(j, carry):
        s_odd = scores(2 * j + 1)
        update(s_scr[...], 2 * j)
        s_scr[...] = scores(2 * j + 2)
        update(s_odd, 2 * j + 1)
        return carry

    lax.fori_loop(0, pairs, trip, 0)
    if n_chunks - 2 * pairs == 1:
        update(s_scr[...], n_chunks - 1)
    else:
        s_odd = scores(n_chunks - 1)
        update(s_scr[...], n_chunks - 2)
        update(s_odd, n_chunks - 1)
    o = acc_scr[...] / l_scr[...]
    if mode == "diff":
        lp = lam_ref[...]
        lam = (jnp.exp(jnp.sum(lp[0:1] * lp[1:2], axis=1, keepdims=True))
               - jnp.exp(jnp.sum(lp[2:3] * lp[3:4], axis=1, keepdims=True)) + lam_init)
        d = o[:tq] - lam * o[tq:]
        o = d * lax.rsqrt(jnp.mean(d * d, axis=-1, keepdims=True) + EPS) * subg_ref[...] * (1.0 - lam_init)
    o_ref[...] = o.astype(o_ref.dtype)


def _flash(mode, q_arr, q_col0, dq, k_arrs, v_arr, v_col0, dv, *, heads, q_row0, n_q, kv_row0, nk,
           tq, tk, scale, out_cols, extra=(), lam_init=0.0):
    qb0 = q_row0 // tq
    kb0 = kv_row0 // nk
    qc0 = q_col0 // dq
    in_specs = [pl.BlockSpec((tq, dq), lambda h, i: (qb0 + i, qc0 + h))]
    args = [q_arr]
    for arr, col0, per_head in k_arrs:
        kc0 = col0 // LANE
        in_specs.append(pl.BlockSpec((nk, LANE), lambda h, i, kc0=kc0, ph=per_head: (kb0, kc0 + h * ph)))
        args.append(arr)
    vc0 = v_col0 // dv
    in_specs.append(pl.BlockSpec((nk, dv), lambda h, i: (kb0, vc0 + h)))
    args.append(v_arr)
    for e in extra:
        in_specs.append(pl.BlockSpec(e.shape, lambda h, i: (0, 0)))
        args.append(e)
    rows = 2 * tq if mode == "diff" else tq
    scratch = []
    if mode == "cat":
        scratch.append(pltpu.VMEM((nk, 2 * LANE), BF16))
    scratch += [pltpu.VMEM((rows, 1), F32), pltpu.VMEM((rows, 1), F32), pltpu.VMEM((rows, dv), F32),
                pltpu.VMEM((rows, tk), F32)]
    body = functools.partial(_flash_body, mode=mode, tq=tq, nk=nk, tk=tk,
                             scale2=None if scale is None else scale * LOG2E, lam_init=lam_init)
    return pl.pallas_call(
        body, grid=(heads, n_q // tq), in_specs=in_specs,
        out_specs=pl.BlockSpec((tq, dv), lambda h, i: (i, h)),
        out_shape=jax.ShapeDtypeStruct((n_q, out_cols), BF16),
        scratch_shapes=scratch,
        compiler_params=_params(("arbitrary", "arbitrary")),
        name="flash_" + mode,
    )(*args)


def _cwin_body(sink_ref, q_ref, k_ref, v_ref, o_ref, *, t, n_lat, n_ctx, with_band):
    i = pl.program_id(0)
    qf = q_ref[...].astype(F32)
    lane = lax.broadcasted_iota(jnp.int32, (t, LANE), 1)
    kc = k_ref[n_lat:n_lat + n_ctx, :]
    vc = v_ref[n_lat:n_lat + n_ctx, :]
    if with_band:
        nb = t + 2 * C_WIN
        q0 = i * t
        kstart = pl.multiple_of(jnp.clip(q0 - C_WIN, 0, n_lat - nb), LANE)
        kb = k_ref[pl.ds(kstart, nb), :]
        vb = v_ref[pl.ds(kstart, nb), :]
        qpos = q0 + (lax.broadcasted_iota(jnp.int32, (C_GROUP * t, nb), 0) & (t - 1))
        kpos = kstart + lax.broadcasted_iota(jnp.int32, (C_GROUP * t, nb), 1)
        valid = jnp.abs(qpos - kpos) <= C_WIN
    grp = lax.shift_right_logical(lax.broadcasted_iota(jnp.int32, (C_GROUP * t, 1), 0), int(math.log2(t)))
    outs = [None] * C_HEADS
    for j in range(C_KVH):
        in_head = (lane >= j * C_HD) & (lane < (j + 1) * C_HD)
        qs = []
        for g in range(C_GROUP):
            hq = j * C_GROUP + g
            chunk = qf[:, (hq // 2) * LANE:(hq // 2 + 1) * LANE]
            if hq % 2 != j:
                chunk = pltpu.roll(chunk, C_HD, 1)
            qs.append(jnp.where(in_head, chunk, 0.0).astype(BF16))
        qj = jnp.concatenate(qs, axis=0)
        sink = jnp.zeros((C_GROUP * t, 1), F32)
        for g in range(C_GROUP):
            sink = jnp.where(grp == g, sink_ref[j * C_GROUP + g] * LOG2E, sink)
        sc = _dot_nt(qj, kc)
        m = jnp.maximum(jnp.max(sc, axis=1, keepdims=True), sink)
        if with_band:
            sb = jnp.where(valid, _dot_nt(qj, kb), NEG)
            m = jnp.maximum(m, jnp.max(sb, axis=1, keepdims=True))
            pb = jnp.exp2(sb - m)
        pc = jnp.exp2(sc - m)
        l = jnp.sum(pc, axis=1, keepdims=True) + jnp.exp2(sink - m)
        o = jnp.dot(pc.astype(BF16), vc, preferred_element_type=F32)
        if with_band:
            l = l + jnp.sum(pb, axis=1, keepdims=True)
            o = o + jnp.dot(pb.astype(BF16), vb, preferred_element_type=F32)
        o = o / l
        for g in range(C_GROUP):
            og = o[g * t:(g + 1) * t]
            if g % 2 != j:
                og = pltpu.roll(og, C_HD, 1)
            outs[j * C_GROUP + g] = og
    cols = [jnp.where(lane < C_HD, outs[2 * c], outs[2 * c + 1]) for c in range(C_HEADS // 2)]
    o_ref[...] = jnp.concatenate(cols, axis=1).astype(o_ref.dtype)


def _cwin(p, sink, *, n_lat, n_ctx, q_row0, n_q, with_band):
    ntok = p.shape[0]
    t = 256 if with_band else n_q
    qb0 = q_row0 // t
    qw = C_HEADS * C_HD
    assert t & (t - 1) == 0
    body = functools.partial(_cwin_body, t=t, n_lat=n_lat, n_ctx=n_ctx, with_band=with_band)
    return pl.pallas_call(
        body, grid=(n_q // t,),
        in_specs=[pl.BlockSpec(memory_space=pltpu.SMEM),
                  pl.BlockSpec((t, qw), lambda i: (qb0 + i, COL_C_Q // qw)),
                  pl.BlockSpec((ntok, LANE), lambda i: (0, COL_C_K // LANE)),
                  pl.BlockSpec((ntok, LANE), lambda i: (0, COL_C_V // LANE))],
        out_specs=pl.BlockSpec((t, qw), lambda i: (i, 0)),
        out_shape=jax.ShapeDtypeStruct((n_q, qw), BF16),
        compiler_params=_params(("arbitrary",)),
        name="window_attn",
    )(sink, p, p, p)


D_QROWS = 8
D_KROWS = 16


def _dna_window_row(i, rows):
    return jnp.clip(D_QROWS * i - NA_ROWS // 2, 0, rows - D_KROWS)


def _dna_body(q_ref, k_ref, v_ref, bias_ref, o_ref, *, n_lat, n_ctx, rows, scale):
    i = pl.program_id(1)
    nkw = D_KROWS * GRID_W
    kstart = pl.multiple_of(_dna_window_row(i, rows) * GRID_W, LANE)
    q = q_ref[...]
    kw = k_ref[pl.ds(kstart, nkw), :]
    vw = v_ref[pl.ds(kstart, nkw), :]
    kc = k_ref[n_lat:n_lat + n_ctx, :]
    vc = v_ref[n_lat:n_lat + n_ctx, :]
    sw = _dot_nt(q, kw) * scale + bias_ref[0, 0]
    sc = _dot_nt(q, kc) * scale
    m = jnp.maximum(jnp.max(sw, axis=1, keepdims=True), jnp.max(sc, axis=1, keepdims=True))
    pw = jnp.exp(sw - m)
    pc = jnp.exp(sc - m)
    l = jnp.sum(pw, axis=1, keepdims=True) + jnp.sum(pc, axis=1, keepdims=True)
    o = (jnp.dot(pw.astype(BF16), vw, preferred_element_type=F32)
         + jnp.dot(pc.astype(BF16), vc, preferred_element_type=F32))
    o_ref[...] = (o / l).astype(o_ref.dtype)


def _dna_bias_table(rpb, rows):
    ri, kr, c, kcol = np.arange(D_QROWS), np.arange(D_KROWS), np.arange(GRID_W), np.arange(GRID_W)
    cs = np.clip(c - NA_COLS // 2, 0, GRID_W - NA_COLS)
    col_ok = (kcol[None, :] >= cs[:, None]) & (kcol[None, :] < cs[:, None] + NA_COLS)
    col_idx = np.clip(kcol[None, :] - c[:, None] + NA_COLS - 1, 0, 2 * NA_COLS - 2)
    col_sel = np.zeros((GRID_W, GRID_W, 2 * NA_COLS - 1), np.float32)
    col_sel[c[:, None], kcol[None, :], col_idx] = col_ok
    row_sel = np.zeros((3, D_QROWS, D_KROWS, 2 * NA_ROWS - 1), np.float32)
    ok = np.zeros((3, D_QROWS, GRID_W, D_KROWS, GRID_W), bool)
    for v, (d, rel) in enumerate(((0, np.maximum(ri - 4, 0)), (4, ri), (8, np.minimum(ri + 4, 8)))):
        row_ok = (kr[None, :] >= rel[:, None]) & (kr[None, :] < rel[:, None] + NA_ROWS)
        row_idx = np.clip(kr[None, :] - d - ri[:, None] + NA_ROWS - 1, 0, 2 * NA_ROWS - 2)
        row_sel[v, ri[:, None], kr[None, :], row_idx] = row_ok
        ok[v] = row_ok[:, None, :, None] & col_ok[None, :, None, :]
    hi = lax.Precision.HIGHEST
    t1 = jnp.einsum('vrkd,hdj->vhrkj', jnp.asarray(row_sel), rpb, precision=hi)
    t2 = jnp.einsum('vhrkj,cnj->hvrckn', t1, jnp.asarray(col_sel), precision=hi)
    tbl = jnp.where(jnp.asarray(ok)[None], t2, NEG)
    return tbl.reshape(rpb.shape[0], 3, D_QROWS * GRID_W, D_KROWS * GRID_W)


def _dna(p, bias_tbl, *, n_lat, n_ctx):
    ntok = p.shape[0]
    rows = n_lat // GRID_W
    nq, nkw = D_QROWS * GRID_W, D_KROWS * GRID_W

    def variant(i):
        return (D_QROWS * i - _dna_window_row(i, rows)) // 4

    body = functools.partial(_dna_body, n_lat=n_lat, n_ctx=n_ctx, rows=rows, scale=D_HD ** -0.5)
    return pl.pallas_call(
        body, grid=(D_HEADS, rows // D_QROWS),
        in_specs=[pl.BlockSpec((nq, D_HD), lambda h, i: (i, COL_D_Q // D_HD + h)),
                  pl.BlockSpec((ntok, D_HD), lambda h, i: (0, COL_D_K // D_HD + h)),
                  pl.BlockSpec((ntok, D_HD), lambda h, i: (0, COL_D_V // D_HD + h)),
                  pl.BlockSpec((1, 1, nq, nkw), lambda h, i: (h, variant(i), 0, 0))],
        out_specs=pl.BlockSpec((nq, D_HD), lambda h, i: (i, h)),
        out_shape=jax.ShapeDtypeStruct((n_lat, D_HEADS * D_HD), BF16),
        compiler_params=_params(("arbitrary", "arbitrary")),
        name="neighbourhood_attn",
    )(p, p, p, bias_tbl)


def _merge_body(x_ref, g_ref, mod_ref, oa_ref, ob_ref, oc_ref, od_ref, wg_ref, bg_ref, wb_ref, o_ref, h_scr,
                *, n_lat, tm):
    i, j = pl.program_id(0), pl.program_id(1)

    @pl.when(j == 0)
    def _():
        x = x_ref[...]
        y = x * lax.rsqrt(jnp.mean(x * x, axis=-1, keepdims=True) + EPS) * g_ref[...]
        shift = _row_select(i, tm, n_lat, mod_ref[0:1, :], mod_ref[1:2, :])
        scale = _row_select(i, tm, n_lat, mod_ref[2:3, :], mod_ref[3:4, :])
        h_scr[...] = (y * (1.0 + scale) + shift).astype(BF16)

    h = h_scr[...]
    merged = None
    for b, o_b in enumerate((oa_ref, ob_ref, oc_ref, od_ref)):
        z = jnp.dot(h, wg_ref[b], preferred_element_type=F32) + bg_ref[b]
        gate = 1.0 / (1.0 + jnp.exp(-z))
        term = gate * jnp.dot(o_b[...], wb_ref[b], preferred_element_type=F32)
        merged = term if merged is None else merged + term
    o_ref[...] = merged.astype(o_ref.dtype)


def _merge(x, g, mod, branches, wg, bg, wb, *, n_lat, tm, tn):
    m, d = x.shape
    body = functools.partial(_merge_body, n_lat=n_lat, tm=tm)
    bspec = pl.BlockSpec((tm, BR_W), lambda i, j: (i, 0))
    return pl.pallas_call(
        body, grid=(m // tm, d // tn),
        in_specs=[pl.BlockSpec((tm, d), lambda i, j: (i, 0)),
                  pl.BlockSpec((1, d), lambda i, j: (0, 0)),
                  pl.BlockSpec((4, d), lambda i, j: (0, 0)),
                  bspec, bspec, bspec, bspec,
                  pl.BlockSpec((N_BR, d, tn), lambda i, j: (0, 0, j)),
                  pl.BlockSpec((N_BR, 1, tn), lambda i, j: (0, 0, j)),
                  pl.BlockSpec((N_BR, BR_W, tn), lambda i, j: (0, 0, j))],
        out_specs=pl.BlockSpec((tm, tn), lambda i, j: (i, j)),
        out_shape=jax.ShapeDtypeStruct((m, d), BF16),
        scratch_shapes=[pltpu.VMEM((tm, d), BF16)],
        compiler_params=_params(("arbitrary", "arbitrary")),
        name="gated_merge",
    )(x, g.reshape(1, d), mod, *branches, wg, bg, wb)


def _resmm_body(a_ref, w_ref, res_ref, gate_ref, o_ref, *, n_lat, tm):
    i = pl.program_id(0)
    gate = _row_select(i, tm, n_lat, gate_ref[0:1, :], gate_ref[1:2, :])
    o_ref[...] = res_ref[...] + gate * jnp.dot(a_ref[...], w_ref[...], preferred_element_type=F32)


def _resmm(a, w, res, gate, *, n_lat, tm, tn):
    m, k = a.shape
    n = w.shape[1]
    body = functools.partial(_resmm_body, n_lat=n_lat, tm=tm)
    return pl.pallas_call(
        body, grid=(m // tm, n // tn),
        in_specs=[pl.BlockSpec((tm, k), lambda i, j: (i, 0)),
                  pl.BlockSpec((k, tn), lambda i, j: (0, j)),
                  pl.BlockSpec((tm, tn), lambda i, j: (i, j)),
                  pl.BlockSpec((2, tn), lambda i, j: (0, j))],
        out_specs=pl.BlockSpec((tm, tn), lambda i, j: (i, j)),
        out_shape=jax.ShapeDtypeStruct((m, n), F32),
        compiler_params=_params(("arbitrary", "arbitrary")),
        name="residual_matmul",
    )(a, w, res, gate)


P_RANKS = P_TOPK + 1
P_PAIRS = tuple((a, b) for a in range(P_RANKS) for b in range(P_RANKS) if (a + 1) * (b + 1) <= P_RANKS)


def _peer_topk_body(q_ref, knh_ref, khn_ref, ae_ref, be_ref, t_ref, cur_scr, top_scr, cand_scr, sum_scr, *, tt):
    q = q_ref[...]
    hk = P_HEADS * P_KEYS
    s_hn = []
    for p in range(2):
        qp = q[:, p * hk:(p + 1) * hk]
        cur_scr[...] = _dot_nt(knh_ref[p], qp).reshape(P_KEYS, P_HEADS, tt)
        s_hn.append(_dot_nt(khn_ref[p], qp).reshape(P_HEADS, P_KEYS, tt))

        def rank(r, carry, p=p):
            c = cur_scr[...]
            mx = jnp.max(c, axis=0)
            cur_scr[...] = jnp.where(c == mx[None], NEG, c)
            top_scr[p * P_RANKS + r] = mx
            return carry

        lax.fori_loop(0, P_RANKS, rank, 0)

    for n, (a, b) in enumerate(P_PAIRS):
        cand_scr[n] = top_scr[a] + top_scr[P_RANKS + b]

    def rank_sum(r, carry):
        c = cand_scr[...]
        mx = jnp.max(c, axis=0)
        cand_scr[...] = jnp.where(c == mx[None], NEG, c)
        sum_scr[r] = mx
        return carry

    lax.fori_loop(0, P_RANKS, rank_sum, 0)

    t0 = sum_scr[0]
    z = jnp.zeros_like(t0)
    for r in range(P_TOPK):
        z = z + jnp.exp(sum_scr[r] - t0)
    inv_z = 1.0 / z
    t_ref[...] = jnp.exp(0.5 * (sum_scr[P_TOPK - 1] + sum_scr[P_TOPK]) - t0) * inv_z
    mx0, mx1 = top_scr[0], top_scr[P_RANKS]
    ae_ref[...] = (jnp.exp(s_hn[0] - mx0[:, None, :]) * inv_z[:, None, :]).reshape(hk, tt)
    be_ref[...] = jnp.exp(s_hn[1] - mx1[:, None, :]).reshape(hk, tt)


def _peer_topk(q, knh, khn, *, tt):
    m = q.shape[0]
    hk = P_HEADS * P_KEYS
    body = functools.partial(_peer_topk_body, tt=tt)
    return pl.pallas_call(
        body, grid=(m // tt,),
        in_specs=[pl.BlockSpec((tt, 2 * hk), lambda i: (i, 0)),
                  pl.BlockSpec((2, hk, hk), lambda i: (0, 0, 0)),
                  pl.BlockSpec((2, hk, hk), lambda i: (0, 0, 0))],
        out_specs=[pl.BlockSpec((hk, tt), lambda i: (0, i)),
                   pl.BlockSpec((hk, tt), lambda i: (0, i)),
                   pl.BlockSpec((P_HEADS, tt), lambda i: (0, i))],
        out_shape=[jax.ShapeDtypeStruct((hk, m), F32), jax.ShapeDtypeStruct((hk, m), F32),
                   jax.ShapeDtypeStruct((P_HEADS, m), F32)],
        scratch_shapes=[pltpu.VMEM((P_KEYS, P_HEADS, tt), F32),
                        pltpu.VMEM((2 * P_RANKS, P_HEADS, tt), F32),
                        pltpu.VMEM((len(P_PAIRS), P_HEADS, tt), F32),
                        pltpu.VMEM((P_RANKS, P_HEADS, tt), F32)],
        compiler_params=_params(("arbitrary",)),
        name="peer_topk",
    )(q, knh, khn)


def _peer_dense_body(u_ref, ht_ref, vt_ref, ae_ref, be_ref, t_ref, acc_ref, p_scr, *, te):
    e = pl.program_id(1)

    @pl.when(e == 0)
    def _():
        acc_ref[...] = jnp.zeros(acc_ref.shape, F32)

    act = jnp.dot(u_ref[...], ht_ref[...], preferred_element_type=F32)
    act = 0.5 * act * (1.0 + lax.erf(act * math.sqrt(0.5)))
    n_i = te // P_KEYS
    for ii in range(n_i):
        i = e * n_i + ii
        w = None
        for h in range(P_HEADS):
            a_row = ae_ref[pl.ds(h * P_KEYS + i, 1), :]
            prod = a_row * be_ref[h * P_KEYS:(h + 1) * P_KEYS, :]
            term = jnp.where(prod >= t_ref[h:h + 1, :], prod, 0.0)
            w = term if w is None else w + term
        p_scr[ii * P_KEYS:(ii + 1) * P_KEYS, :] = (w * act[ii * P_KEYS:(ii + 1) * P_KEYS, :]).astype(BF16)
    acc_ref[...] += jnp.dot(vt_ref[...], p_scr[...], preferred_element_type=F32)


def _peer_dense(u, ht, vt, ae, be, thr, *, tm, te):
    n_exp, d = u.shape
    m = ht.shape[1]
    hk = P_HEADS * P_KEYS
    body = functools.partial(_peer_dense_body, te=te)
    return pl.pallas_call(
        body, grid=(m // tm, n_exp // te),
        in_specs=[pl.BlockSpec((te, d), lambda t, e: (e, 0)),
                  pl.BlockSpec((d, tm), lambda t, e: (0, t)),
                  pl.BlockSpec((d, te), lambda t, e: (0, e)),
                  pl.BlockSpec((hk, tm), lambda t, e: (0, t)),
                  pl.BlockSpec((hk, tm), lambda t, e: (0, t)),
                  pl.BlockSpec((P_HEADS, tm), lambda t, e: (0, t))],
        out_specs=pl.BlockSpec((d, tm), lambda t, e: (0, t)),
        out_shape=jax.ShapeDtypeStruct((d, m), F32),
        scratch_shapes=[pltpu.VMEM((te, tm), BF16)],
        compiler_params=_params(("arbitrary", "arbitrary")),
        name="peer_dense",
    )(u, ht, vt, ae, be, thr)


def _peer_out_body(x_ref, at_ref, gate_ref, o_ref, *, n_lat, tm):
    gate = _row_select(pl.program_id(0), tm, n_lat, gate_ref[0:1, :], gate_ref[1:2, :])
    o_ref[...] = x_ref[...] + gate * at_ref[...].T


def _peer_out(x, acc_t, gate, *, n_lat, tm):
    m, d = x.shape
    body = functools.partial(_peer_out_body, n_lat=n_lat, tm=tm)
    return pl.pallas_call(
        body, grid=(m // tm,),
        in_specs=[pl.BlockSpec((tm, d), lambda i: (i, 0)),
                  pl.BlockSpec((d, tm), lambda i: (0, i)),
                  pl.BlockSpec((2, d), lambda i: (0, 0))],
        out_specs=pl.BlockSpec((tm, d), lambda i: (i, 0)),
        out_shape=jax.ShapeDtypeStruct((m, d), F32),
        compiler_params=_params(("arbitrary",)),
        name="peer_residual",
    )(x, acc_t, gate)


def _final_norm_body(x_ref, g_ref, o_ref):
    x = x_ref[...]
    o_ref[...] = x * lax.rsqrt(jnp.mean(x * x, axis=-1, keepdims=True) + EPS) * g_ref[...]


def _final_norm(x, g, *, n_rows, tm):
    d = x.shape[1]
    return pl.pallas_call(
        _final_norm_body, grid=(n_rows // tm,),
        in_specs=[pl.BlockSpec((tm, d), lambda i: (i, 0)), pl.BlockSpec((1, d), lambda i: (0, 0))],
        out_specs=pl.BlockSpec((tm, d), lambda i: (i, 0)),
        out_shape=jax.ShapeDtypeStruct((n_rows, d), F32),
        compiler_params=_params(("arbitrary",)),
        name="final_norm",
    )(x, g.reshape(1, d))


def _rope_tables(n_lat, n_ctx):
    t = jnp.arange(n_lat, dtype=jnp.int32)
    row = (t // GRID_W).astype(F32)
    col = (t % GRID_W).astype(F32)
    n_freq = ROPE_DIM // 4
    inv = ROPE_THETA ** (-jnp.arange(n_freq, dtype=F32) / n_freq)
    ang = jnp.concatenate([row[:, None] * inv, col[:, None] * inv], axis=-1)
    cos, sin = jnp.cos(ang), jnp.sin(ang)
    cos128 = jnp.concatenate([jnp.tile(cos, (1, 4)), jnp.ones((n_ctx, LANE), F32)], axis=0)
    sin128 = jnp.concatenate([jnp.tile(jnp.concatenate([-sin, sin], axis=1), (1, 2)),
                              jnp.zeros((n_ctx, LANE), F32)], axis=0)
    return cos128, sin128


def _rope_flags(width, ranges):
    f = np.zeros((1, width), np.float32)
    for lo, hi in ranges:
        f[0, lo:hi] = 1.0
    return jnp.asarray(f)


def _pad_heads(w, heads, width, padded):
    k = w.shape[0]
    w = w.reshape(k, heads, width)
    return jnp.pad(w, ((0, 0), (0, 0), (0, padded - width))).reshape(k, heads * padded)


def kernel(x, c, ctx, c_ctx, ada_w, ada_b, norm_mix, norm_ffn, w_in, a_lam_q1, a_lam_k1, a_lam_q2, a_lam_k2,
           a_subln, b_qa_norm, b_w_uq, b_kva_norm, b_w_ukv, c_sink, d_rpb, w_branch, w_gate, b_gate, w_out,
           peer_wq, peer_keys, peer_u, peer_v, final_norm):
    assert x.shape[0] == 1 and ctx.shape[0] == 1
    n_lat, d = x.shape[1], x.shape[2]
    n_ctx = ctx.shape[1]
    depth = ada_w.shape[0]
    ntok = n_lat + n_ctx
    rows = n_lat // GRID_W
    assert n_lat % n_ctx == 0 and n_lat % (D_QROWS * GRID_W) == 0 and rows >= D_KROWS

    tm = _tile(ntok, 768)
    xs = jnp.concatenate([x[0], ctx[0]], axis=0)
    cos128, sin128 = _rope_tables(n_lat, n_ctx)
    in_flags = _rope_flags(IN_PAD, [(COL_A_Q, COL_A_V), (COL_B_KPE, COL_B_KPE + B_ROPE), (COL_C_Q, COL_C_V)])
    uq_flags = _rope_flags(B_HEADS * 2 * LANE, [(h * 2 * LANE + B_NOPE, h * 2 * LANE + B_NOPE + B_ROPE)
                                                for h in range(B_HEADS)])

    c8 = jnp.zeros((8, d), F32).at[0].set(c[0]).at[1].set(c_ctx)
    mods = _ada(c8, ada_w, ada_b)

    b_scale2 = (B_NOPE + B_ROPE) ** -0.5 * LOG2E
    ks = np.ones((1, IN_PAD), np.float32)
    ks[0, COL_A_K:COL_A_V] = A_QK ** -0.5 * LOG2E
    ks[0, COL_B_KPE:COL_B_KPE + B_ROPE] = b_scale2
    ks[0, COL_C_K:COL_C_V] = C_HD ** -0.5 * LOG2E
    key_scale = jnp.asarray(ks)

    eye = jnp.eye(P_HEADS, dtype=F32)
    for l in range(depth):
        lam_init = 0.8 - 0.6 * math.exp(-0.3 * l)
        mod6 = mods[l, :2].reshape(2, 6, d)
        mod1 = jnp.concatenate([mod6[:, 0], mod6[:, 1]], axis=0)
        mod2 = jnp.concatenate([mod6[:, 3], mod6[:, 4]], axis=0)
        gate1, gate2 = mod6[:, 2], mod6[:, 5]

        w = w_in[l]
        k_pe_end = COL_B_KPE + B_ROPE
        w_in_p = jnp.concatenate([w[:, :k_pe_end], jnp.zeros((d, COL_C_Q - k_pe_end), F32), w[:, k_pe_end:],
                                  jnp.zeros((d, IN_PAD - (COL_D_V + D_HEADS * D_HD)), F32)], axis=1)
        w_in_p = (w_in_p * key_scale).astype(BF16)
        w_uq_p = _pad_heads(b_w_uq[l], B_HEADS, B_NOPE + B_ROPE, 2 * LANE).astype(BF16)
        w_ukv = b_w_ukv[l].reshape(B_KVL, B_HEADS, B_NOPE + B_V)
        w_ukv_p = jnp.concatenate([w_ukv[:, :, :B_NOPE].reshape(B_KVL, -1) * b_scale2,
                                   w_ukv[:, :, B_NOPE:].reshape(B_KVL, -1)], axis=1).astype(BF16)
        wq_p = peer_wq[l].reshape(d, P_HEADS, 2, P_KD // 2).transpose(0, 2, 1, 3).reshape(d, -1).astype(BF16)
        keys = peer_keys[l]
        knh = jnp.stack([jnp.einsum('hnd,hg->nhgd', keys[:, p], eye).reshape(P_HEADS * P_KEYS, -1)
                         for p in range(2)]).astype(BF16)
        khn = jnp.stack([jnp.einsum('hnd,hg->hngd', keys[:, p], eye).reshape(P_HEADS * P_KEYS, -1)
                         for p in range(2)]).astype(BF16)
        u_b = peer_u[l].astype(BF16)
        vt_b = peer_v[l].T.astype(BF16)

        p = _nmm(xs, 0, d, norm_mix[l], mod1, w_in_p, n_lat=n_lat, tm=tm, tn=_tile(IN_PAD, 1024),
                 out_dtype=BF16, rope=(cos128, sin128, in_flags))
        qb = _nmm(p, COL_B_QL // B_QL, B_QL, b_qa_norm[l], None, w_uq_p, n_lat=n_lat, tm=tm, tn=w_uq_p.shape[1],
                  out_dtype=BF16, rope=(cos128, sin128, uq_flags))
        kvb = _nmm(p, COL_B_KVL // B_KVL, B_KVL, b_kva_norm[l], None, w_ukv_p, n_lat=n_lat, tm=tm,
                   tn=w_ukv_p.shape[1], out_dtype=BF16)
        lam_p = jnp.stack([a_lam_q1[l], a_lam_k1[l], a_lam_q2[l], a_lam_k2[l]])
        subg = a_subln[l].reshape(1, A_V)
        bias_tbl = _dna_bias_table(d_rpb[l].astype(F32), rows)

        def attend(q_row0, n_q, kv_row0, nk, tq, tk):
            oa = _flash("diff", p, COL_A_Q, LANE, [(p, COL_A_K, 1)], p, COL_A_V, A_V, heads=A_HEADS,
                        q_row0=q_row0, n_q=n_q, kv_row0=kv_row0, nk=nk, tq=tq, tk=tk, scale=None,
                        out_cols=A_HEADS * A_V, extra=(lam_p, subg), lam_init=lam_init)
            ob = _flash("cat", qb, 0, 2 * LANE, [(kvb, 0, 1), (p, COL_B_KPE, 0)], kvb, B_HEADS * B_NOPE, B_V,
                        heads=B_HEADS, q_row0=q_row0, n_q=n_q, kv_row0=kv_row0, nk=nk, tq=tq, tk=tk,
                        scale=None, out_cols=B_HEADS * B_V)
            return oa, ob

        tk = _tile(ntok, 768)
        oa, ob = attend(0, n_lat, 0, ntok, _tile(n_lat, 512), tk)
        oc = _cwin(p, c_sink[l], n_lat=n_lat, n_ctx=n_ctx, q_row0=0, n_q=n_lat, with_band=True)
        od = _dna(p, bias_tbl, n_lat=n_lat, n_ctx=n_ctx)
        if l < depth - 1:
            oac, obc = attend(n_lat, n_ctx, n_lat, n_ctx, n_ctx, n_ctx)
            occ = _cwin(p, c_sink[l], n_lat=n_lat, n_ctx=n_ctx, q_row0=n_lat, n_q=n_ctx, with_band=False)
            odc = _flash("plain", p, COL_D_Q, D_HD, [(p, COL_D_K, 1)], p, COL_D_V, D_HD, heads=D_HEADS,
                         q_row0=n_lat, n_q=n_ctx, kv_row0=n_lat, nk=n_ctx, tq=n_ctx, tk=n_ctx,
                         scale=D_HD ** -0.5, out_cols=D_HEADS * D_HD)
            branches = [jnp.concatenate([a, b], axis=0) for a, b in ((oa, oac), (ob, obc), (oc, occ), (od, odc))]
        else:
            branches = [jnp.pad(a, ((0, n_ctx), (0, 0))) for a in (oa, ob, oc, od)]
        merged = _merge(xs, norm_mix[l], mod1, branches, w_gate[l].astype(BF16), b_gate[l].reshape(N_BR, 1, d),
                        w_branch[l].astype(BF16), n_lat=n_lat, tm=tm, tn=_tile(d, 256))
        xs = _resmm(merged, w_out[l].astype(BF16), xs, gate1, n_lat=n_lat, tm=tm, tn=_tile(d, 1024))

        qp, h_t = _nmm(xs, 0, d, norm_ffn[l], mod2, wq_p, n_lat=n_lat, tm=tm, tn=_tile(d, 1024), out_dtype=BF16,
                       want_ht=True)
        ae, be, thr = _peer_topk(qp, knh, khn, tt=_tile(ntok, 256))
        acc_t = _peer_dense(u_b, h_t, vt_b, ae, be, thr, tm=tm, te=512)
        xs = _peer_out(xs, acc_t, gate2, n_lat=n_lat, tm=_tile(ntok, 256))

    out = _final_norm(xs, final_norm, n_rows=n_lat, tm=_tile(n_lat, 512))
    return out[None]
```

```python
import functools
import math

import numpy as np
import jax
import jax.numpy as jnp
from jax import lax
from jax.experimental import pallas as pl
from jax.experimental.pallas import tpu as pltpu

F32 = jnp.float32
BF16 = jnp.bfloat16

GRID_W = 64
ROPE_THETA = 10000.0
ROPE_DIM = 64
EPS = 1e-6
NEG = -1e30
LOG2E = 1.4426950408889634

A_HEADS, A_QK, A_V = 4, 64, 128
B_HEADS, B_QL, B_KVL, B_NOPE, B_ROPE, B_V = 4, 512, 256, 128, 64, 128
C_HEADS, C_KVH, C_GROUP, C_HD, C_WIN = 8, 2, 4, 64, 128
D_HEADS, D_HD, NA_ROWS, NA_COLS = 4, 128, 8, 16
N_BR, BR_W = 4, 512
P_HEADS, P_KEYS, P_KD, P_TOPK = 8, 128, 256, 16

LANE = 128
VMEM_LIMIT = 56 * 1024 * 1024

COL_A_Q, COL_A_K, COL_A_V = 0, 512, 1024
COL_B_QL, COL_B_KVL, COL_B_KPE = 1536, 2048, 2304
COL_C_Q, COL_C_K, COL_C_V = 2560, 3072, 3200
COL_D_Q, COL_D_K, COL_D_V = 3328, 3840, 4352
IN_PAD = 5120


def _tile(n, target, mult=LANE):
    best = None
    for t in range(mult, min(n, target) + 1, mult):
        if n % t == 0:
            best = t
    assert best is not None, (n, target)
    return best


def _params(sem):
    return pltpu.CompilerParams(dimension_semantics=sem, vmem_limit_bytes=VMEM_LIMIT)


def _dot_nt(a, b):
    return lax.dot_general(a, b, (((1,), (1,)), ((), ())), preferred_element_type=F32)


def _row_select(i, tm, n_lat, lat_row, ctx_row):
    rows = i * tm + lax.broadcasted_iota(jnp.int32, (tm, 1), 0)
    return jnp.where(rows >= n_lat, ctx_row, lat_row)


def _swap_halves64(x):
    outs = []
    lane = lax.broadcasted_iota(jnp.int32, (x.shape[0], LANE), 1) & 63
    for c in range(x.shape[1] // LANE):
        xc = x[:, c * LANE:(c + 1) * LANE]
        outs.append(jnp.where(lane < 32, pltpu.roll(xc, LANE - 32, 1), pltpu.roll(xc, 32, 1)))
    return outs[0] if len(outs) == 1 else jnp.concatenate(outs, axis=1)


def _ada_body(c_ref, w_ref, b_ref, o_ref):
    c = c_ref[...]
    s = c * (1.0 / (1.0 + jnp.exp(-c)))
    o_ref[0] = jnp.dot(s.astype(BF16), w_ref[0].astype(BF16), preferred_element_type=F32) + b_ref[0]


def _ada(c8, ada_w, ada_b):
    depth, d, n = ada_w.shape
    tn = _tile(n, 1024)
    return pl.pallas_call(
        _ada_body,
        grid=(depth, n // tn),
        in_specs=[pl.BlockSpec((8, d), lambda l, j: (0, 0)),
                  pl.BlockSpec((1, d, tn), lambda l, j: (l, 0, j)),
                  pl.BlockSpec((1, 1, tn), lambda l, j: (l, 0, j))],
        out_specs=pl.BlockSpec((1, 8, tn), lambda l, j: (l, 0, j)),
        out_shape=jax.ShapeDtypeStruct((depth, 8, n), F32),
        compiler_params=_params(("arbitrary", "arbitrary")),
        name="ada",
    )(c8, ada_w, ada_b.reshape(depth, 1, n))


def _nmm_body(*refs, n_lat, tm, has_mod, rope, want_ht):
    it = iter(refs)
    x_ref, g_ref = next(it), next(it)
    mod_ref = next(it) if has_mod else None
    w_ref = next(it)
    if rope:
        cos_ref, sin_ref, flag_ref = next(it), next(it), next(it)
    o_ref = next(it)
    ht_ref = next(it) if want_ht else None
    h_scr = next(it)
    i, j = pl.program_id(0), pl.program_id(1)

    @pl.when(j == 0)
    def _():
        x = x_ref[...].astype(F32)
        y = x * lax.rsqrt(jnp.mean(x * x, axis=-1, keepdims=True) + EPS) * g_ref[...]
        if has_mod:
            shift = _row_select(i, tm, n_lat, mod_ref[0:1, :], mod_ref[1:2, :])
            scale = _row_select(i, tm, n_lat, mod_ref[2:3, :], mod_ref[3:4, :])
            y = y * (1.0 + scale) + shift
        h_scr[...] = y.astype(BF16)
        if want_ht:
            ht_ref[...] = y.T.astype(BF16)

    acc = jnp.dot(h_scr[...], w_ref[...], preferred_element_type=F32)
    if rope:
        reps = acc.shape[1] // LANE
        flag = flag_ref[...]
        cos = 1.0 + flag * (jnp.tile(cos_ref[...], (1, reps)) - 1.0)
        sin = flag * jnp.tile(sin_ref[...], (1, reps))
        acc = acc * cos + _swap_halves64(acc) * sin
    o_ref[...] = acc.astype(o_ref.dtype)


def _nmm(x, xcol, kc, g, mod, w, *, n_lat, tm, tn, out_dtype, rope=None, want_ht=False):
    m = x.shape[0]
    n = w.shape[1]
    in_specs = [pl.BlockSpec((tm, kc), lambda i, j: (i, xcol)),
                pl.BlockSpec((1, kc), lambda i, j: (0, 0))]
    args = [x, g.reshape(1, kc)]
    if mod is not None:
        in_specs.append(pl.BlockSpec((4, kc), lambda i, j: (0, 0)))
        args.append(mod)
    in_specs.append(pl.BlockSpec((kc, tn), lambda i, j: (0, j)))
    args.append(w)
    if rope is not None:
        cos, sin, flags = rope
        in_specs += [pl.BlockSpec((tm, LANE), lambda i, j: (i, 0)),
                     pl.BlockSpec((tm, LANE), lambda i, j: (i, 0)),
                     pl.BlockSpec((1, tn), lambda i, j: (0, j))]
        args += [cos, sin, flags]
    out_specs = [pl.BlockSpec((tm, tn), lambda i, j: (i, j))]
    out_shape = [jax.ShapeDtypeStruct((m, n), out_dtype)]
    if want_ht:
        out_specs.append(pl.BlockSpec((kc, tm), lambda i, j: (0, i)))
        out_shape.append(jax.ShapeDtypeStruct((kc, m), BF16))
    body = functools.partial(_nmm_body, n_lat=n_lat, tm=tm, has_mod=mod is not None,
                             rope=rope is not None, want_ht=want_ht)
    res = pl.pallas_call(
        body, grid=(m // tm, n // tn), in_specs=in_specs, out_specs=out_specs, out_shape=out_shape,
        scratch_shapes=[pltpu.VMEM((tm, kc), BF16)],
        compiler_params=_params(("arbitrary", "arbitrary")),
        name="norm_matmul",
    )(*args)
    return res if want_ht else res[0]


def _flash_body(*refs, mode, tq, nk, tk, scale2, lam_init):
    it = iter(refs)
    q_ref = next(it)
    if mode == "cat":
        kn_ref, kp_ref = next(it), next(it)
    else:
        k_ref = next(it)
    v_ref = next(it)
    if mode == "diff":
        lam_ref, subg_ref = next(it), next(it)
    o_ref = next(it)
    if mode == "cat":
        kcat = next(it)
    m_scr, l_scr, acc_scr, s_scr = next(it), next(it), next(it), next(it)

    q = q_ref[...]
    if mode == "diff":
        lane = lax.broadcasted_iota(jnp.int32, q.shape, 1)
        zero = jnp.zeros_like(q)
        q = jnp.concatenate([jnp.where(lane < A_QK, q, zero), jnp.where(lane >= A_QK, q, zero)], axis=0)
    if mode == "cat":
        @pl.when(pl.program_id(1) == 0)
        def _():
            kcat[:, :LANE] = kn_ref[...]
            kcat[:, LANE:] = kp_ref[...]
        k_src = kcat
    else:
        k_src = k_ref

    m_scr[...] = jnp.full(m_scr.shape, NEG, F32)
    l_scr[...] = jnp.zeros(l_scr.shape, F32)
    acc_scr[...] = jnp.zeros(acc_scr.shape, F32)

    def scores(c):
        off = pl.multiple_of(c * tk, tk)
        s = _dot_nt(q, k_src[pl.ds(off, tk), :])
        return s if scale2 is None else s * scale2

    def update(s, c):
        off = pl.multiple_of(c * tk, tk)
        v = v_ref[pl.ds(off, tk), :]
        m_prev = m_scr[...]
        m_new = jnp.maximum(m_prev, jnp.max(s, axis=1, keepdims=True))
        alpha = jnp.exp2(m_prev - m_new)
        p = jnp.exp2(s - m_new)
        l_scr[...] = alpha * l_scr[...] + jnp.sum(p, axis=1, keepdims=True)
        acc_scr[...] = alpha * acc_scr[...] + jnp.dot(p.astype(BF16), v, preferred_element_type=F32)
        m_scr[...] = m_new

    n_chunks = nk // tk
    pairs = (n_chunks - 1) // 2
    s_scr[...] = scores(0)

    def trip(j, carry):
        s_odd = scores(2 * j + 1)
        update(s_scr[...], 2 * j)
        s_scr[...] = scores(2 * j + 2)
        update(s_odd, 2 * j + 1)
        return carry

    lax.fori_loop(0, pairs, trip, 0)
    if n_chunks - 2 * pairs == 1:
        update(s_scr[...], n_chunks - 1)
    else:
        s_odd = scores(n_chunks - 1)
        update(s_scr[...], n_chunks - 2)
        update(s_odd, n_chunks - 1)
    o = acc_scr[...] / l_scr[...]
    if mode == "diff":
        lp = lam_ref[...]
        lam = (jnp.exp(jnp.sum(lp[0:1] * lp[1:2], axis=1, keepdims=True))
               - jnp.exp(jnp.sum(lp[2:3] * lp[3:4], axis=1, keepdims=True)) + lam_init)
        d = o[:tq] - lam * o[tq:]
        o = d * lax.rsqrt(jnp.mean(d * d, axis=-1, keepdims=True) + EPS) * subg_ref[...] * (1.0 - lam_init)
    o_ref[...] = o.astype(o_ref.dtype)


def _flash(mode, q_arr, q_col0, dq, k_arrs, v_arr, v_col0, dv, *, heads, q_row0, n_q, kv_row0, nk,
           tq, tk, scale, out_cols, extra=(), lam_init=0.0):
    qb0 = q_row0 // tq
    kb0 = kv_row0 // nk
    qc0 = q_col0 // dq
    in_specs = [pl.BlockSpec((tq, dq), lambda h, i: (qb0 + i, qc0 + h))]
    args = [q_arr]
    for arr, col0, per_head in k_arrs:
        kc0 = col0 // LANE
        in_specs.append(pl.BlockSpec((nk, LANE), lambda h, i, kc0=kc0, ph=per_head: (kb0, kc0 + h * ph)))
        args.append(arr)
    vc0 = v_col0 // dv
    in_specs.append(pl.BlockSpec((nk, dv), lambda h, i: (kb0, vc0 + h)))
    args.append(v_arr)
    for e in extra:
        in_specs.append(pl.BlockSpec(e.shape, lambda h, i: (0, 0)))
        args.append(e)
    rows = 2 * tq if mode == "diff" else tq
    scratch = []
    if mode == "cat":
        scratch.append(pltpu.VMEM((nk, 2 * LANE), BF16))
    scratch += [pltpu.VMEM((rows, 1), F32), pltpu.VMEM((rows, 1), F32), pltpu.VMEM((rows, dv), F32),
                pltpu.VMEM((rows, tk), F32)]
    body = functools.partial(_flash_body, mode=mode, tq=tq, nk=nk, tk=tk,
                             scale2=None if scale is None else scale * LOG2E, lam_init=lam_init)
    return pl.pallas_call(
        body, grid=(heads, n_q // tq), in_specs=in_specs,
        out_specs=pl.BlockSpec((tq, dv), lambda h, i: (i, h)),
        out_shape=jax.ShapeDtypeStruct((n_q, out_cols), BF16),
        scratch_shapes=scratch,
        compiler_params=_params(("arbitrary", "arbitrary")),
        name="flash_" + mode,
    )(*args)


def _cwin_body(sink_ref, q_ref, k_ref, v_ref, o_ref, *, t, n_lat, n_ctx, with_band):
    i = pl.program_id(0)
    qf = q_ref[...].astype(F32)
    lane = lax.broadcasted_iota(jnp.int32, (t, LANE), 1)
    kc = k_ref[n_lat:n_lat + n_ctx, :]
    vc = v_ref[n_lat:n_lat + n_ctx, :]
    if with_band:
        nb = t + 2 * C_WIN
        q0 = i * t
        kstart = pl.multiple_of(jnp.clip(q0 - C_WIN, 0, n_lat - nb), LANE)
        kb = k_ref[pl.ds(kstart, nb), :]
        vb = v_ref[pl.ds(kstart, nb), :]
        qpos = q0 + (lax.broadcasted_iota(jnp.int32, (C_GROUP * t, nb), 0) & (t - 1))
        kpos = kstart + lax.broadcasted_iota(jnp.int32, (C_GROUP * t, nb), 1)
        valid = jnp.abs(qpos - kpos) <= C_WIN
    grp = lax.shift_right_logical(lax.broadcasted_iota(jnp.int32, (C_GROUP * t, 1), 0), int(math.log2(t)))
    outs = [None] * C_HEADS
    for j in range(C_KVH):
        in_head = (lane >= j * C_HD) & (lane < (j + 1) * C_HD)
        qs = []
        for g in range(C_GROUP):
            hq = j * C_GROUP + g
            chunk = qf[:, (hq // 2) * LANE:(hq // 2 + 1) * LANE]
            if hq % 2 != j:
                chunk = pltpu.roll(chunk, C_HD, 1)
            qs.append(jnp.where(in_head, chunk, 0.0).astype(BF16))
        qj = jnp.concatenate(qs, axis=0)
        sink = jnp.zeros((C_GROUP * t, 1), F32)
        for g in range(C_GROUP):
            sink = jnp.where(grp == g, sink_ref[j * C_GROUP + g] * LOG2E, sink)
        sc = _dot_nt(qj, kc)
        m = jnp.maximum(jnp.max(sc, axis=1, keepdims=True), sink)
        if with_band:
            sb = jnp.where(valid, _dot_nt(qj, kb), NEG)
            m = jnp.maximum(m, jnp.max(sb, axis=1, keepdims=True))
            pb = jnp.exp2(sb - m)
        pc = jnp.exp2(sc - m)
        l = jnp.sum(pc, axis=1, keepdims=True) + jnp.exp2(sink - m)
        o = jnp.dot(pc.astype(BF16), vc, preferred_element_type=F32)
        if with_band:
            l = l + jnp.sum(pb, axis=1, keepdims=True)
            o = o + jnp.dot(pb.astype(BF16), vb, preferred_element_type=F32)
        o = o / l
        for g in range(C_GROUP):
            og = o[g * t:(g + 1) * t]
            if g % 2 != j:
                og = pltpu.roll(og, C_HD, 1)
            outs[j * C_GROUP + g] = og
    cols = [jnp.where(lane < C_HD, outs[2 * c], outs[2 * c + 1]) for c in range(C_HEADS // 2)]
    o_ref[...] = jnp.concatenate(cols, axis=1).astype(o_ref.dtype)


def _cwin(p, sink, *, n_lat, n_ctx, q_row0, n_q, with_band):
    ntok = p.shape[0]
    t = 256 if with_band else n_q
    qb0 = q_row0 // t
    qw = C_HEADS * C_HD
    assert t & (t - 1) == 0
    body = functools.partial(_cwin_body, t=t, n_lat=n_lat, n_ctx=n_ctx, with_band=with_band)
    return pl.pallas_call(
        body, grid=(n_q // t,),
        in_specs=[pl.BlockSpec(memory_space=pltpu.SMEM),
                  pl.BlockSpec((t, qw), lambda i: (qb0 + i, COL_C_Q // qw)),
                  pl.BlockSpec((ntok, LANE), lambda i: (0, COL_C_K // LANE)),
                  pl.BlockSpec((ntok, LANE), lambda i: (0, COL_C_V // LANE))],
        out_specs=pl.BlockSpec((t, qw), lambda i: (i, 0)),
        out_shape=jax.ShapeDtypeStruct((n_q, qw), BF16),
        compiler_params=_params(("arbitrary",)),
        name="window_attn",
    )(sink, p, p, p)


D_QROWS = 8
D_KROWS = 16


def _dna_window_row(i, rows):
    return jnp.clip(D_QROWS * i - NA_ROWS // 2, 0, rows - D_KROWS)


def _dna_body(q_ref, k_ref, v_ref, bias_ref, o_ref, *, n_lat, n_ctx, rows, scale):
    i = pl.program_id(1)
    nkw = D_KROWS * GRID_W
    kstart = pl.multiple_of(_dna_window_row(i, rows) * GRID_W, LANE)
    q = q_ref[...]
    kw = k_ref[pl.ds(kstart, nkw), :]
    vw = v_ref[pl.ds(kstart, nkw), :]
    kc = k_ref[n_lat:n_lat + n_ctx, :]
    vc = v_ref[n_lat:n_lat + n_ctx, :]
    sw = _dot_nt(q, kw) * scale + bias_ref[0, 0]
    sc = _dot_nt(q, kc) * scale
    m = jnp.maximum(jnp.max(sw, axis=1, keepdims=True), jnp.max(sc, axis=1, keepdims=True))
    pw = jnp.exp(sw - m)
    pc = jnp.exp(sc - m)
    l = jnp.sum(pw, axis=1, keepdims=True) + jnp.sum(pc, axis=1, keepdims=True)
    o = (jnp.dot(pw.astype(BF16), vw, preferred_element_type=F32)
         + jnp.dot(pc.astype(BF16), vc, preferred_element_type=F32))
    o_ref[...] = (o / l).astype(o_ref.dtype)


def _dna_bias_table(rpb, rows):
    ri, kr, c, kcol = np.arange(D_QROWS), np.arange(D_KROWS), np.arange(GRID_W), np.arange(GRID_W)
    cs = np.clip(c - NA_COLS // 2, 0, GRID_W - NA_COLS)
    col_ok = (kcol[None, :] >= cs[:, None]) & (kcol[None, :] < cs[:, None] + NA_COLS)
    col_idx = np.clip(kcol[None, :] - c[:, None] + NA_COLS - 1, 0, 2 * NA_COLS - 2)
    col_sel = np.zeros((GRID_W, GRID_W, 2 * NA_COLS - 1), np.float32)
    col_sel[c[:, None], kcol[None, :], col_idx] = col_ok
    row_sel = np.zeros((3, D_QROWS, D_KROWS, 2 * NA_ROWS - 1), np.float32)
    ok = np.zeros((3, D_QROWS, GRID_W, D_KROWS, GRID_W), bool)
    for v, (d, rel) in enumerate(((0, np.maximum(ri - 4, 0)), (4, ri), (8, np.minimum(ri + 4, 8)))):
        row_ok = (kr[None, :] >= rel[:, None]) & (kr[None, :] < rel[:, None] + NA_ROWS)
        row_idx = np.clip(kr[None, :] - d - ri[:, None] + NA_ROWS - 1, 0, 2 * NA_ROWS - 2)
        row_sel[v, ri[:, None], kr[None, :], row_idx] = row_ok
        ok[v] = row_ok[:, None, :, None] & col_ok[None, :, None, :]
    hi = lax.Precision.HIGHEST
    t1 = jnp.einsum('vrkd,hdj->vhrkj', jnp.asarray(row_sel), rpb, precision=hi)
    t2 = jnp.einsum('vhrkj,cnj->hvrckn', t1, jnp.asarray(col_sel), precision=hi)
    tbl = jnp.where(jnp.asarray(ok)[None], t2, NEG)
    return tbl.reshape(rpb.shape[0], 3, D_QROWS * GRID_W, D_KROWS * GRID_W)


def _dna(p, bias_tbl, *, n_lat, n_ctx):
    ntok = p.shape[0]
    rows = n_lat // GRID_W
    nq, nkw = D_QROWS * GRID_W, D_KROWS * GRID_W

    def variant(i):
        return (D_QROWS * i - _dna_window_row(i, rows)) // 4

    body = functools.partial(_dna_body, n_lat=n_lat, n_ctx=n_ctx, rows=rows, scale=D_HD ** -0.5)
    return pl.pallas_call(
        body, grid=(D_HEADS, rows // D_QROWS),
        in_specs=[pl.BlockSpec((nq, D_HD), lambda h, i: (i, COL_D_Q // D_HD + h)),
                  pl.BlockSpec((ntok, D_HD), lambda h, i: (0, COL_D_K // D_HD + h)),
                  pl.BlockSpec((ntok, D_HD), lambda h, i: (0, COL_D_V // D_HD + h)),
                  pl.BlockSpec((1, 1, nq, nkw), lambda h, i: (h, variant(i), 0, 0))],
        out_specs=pl.BlockSpec((nq, D_HD), lambda h, i: (i, h)),
        out_shape=jax.ShapeDtypeStruct((n_lat, D_HEADS * D_HD), BF16),
        compiler_params=_params(("arbitrary", "arbitrary")),
        name="neighbourhood_attn",
    )(p, p, p, bias_tbl)


def _merge_body(x_ref, g_ref, mod_ref, oa_ref, ob_ref, oc_ref, od_ref, wg_ref, bg_ref, wb_ref, o_ref, h_scr,
                *, n_lat, tm):
    i, j = pl.program_id(0), pl.program_id(1)

    @pl.when(j == 0)
    def _():
        x = x_ref[...]
        y = x * lax.rsqrt(jnp.mean(x * x, axis=-1, keepdims=True) + EPS) * g_ref[...]
        shift = _row_select(i, tm, n_lat, mod_ref[0:1, :], mod_ref[1:2, :])
        scale = _row_select(i, tm, n_lat, mod_ref[2:3, :], mod_ref[3:4, :])
        h_scr[...] = (y * (1.0 + scale) + shift).astype(BF16)

    h = h_scr[...]
    merged = None
    for b, o_b in enumerate((oa_ref, ob_ref, oc_ref, od_ref)):
        z = jnp.dot(h, wg_ref[b], preferred_element_type=F32) + bg_ref[b]
        gate = 1.0 / (1.0 + jnp.exp(-z))
        term = gate * jnp.dot(o_b[...], wb_ref[b], preferred_element_type=F32)
        merged = term if merged is None else merged + term
    o_ref[...] = merged.astype(o_ref.dtype)


def _merge(x, g, mod, branches, wg, bg, wb, *, n_lat, tm, tn):
    m, d = x.shape
    body = functools.partial(_merge_body, n_lat=n_lat, tm=tm)
    bspec = pl.BlockSpec((tm, BR_W), lambda i, j: (i, 0))
    return pl.pallas_call(
        body, grid=(m // tm, d // tn),
        in_specs=[pl.BlockSpec((tm, d), lambda i, j: (i, 0)),
                  pl.BlockSpec((1, d), lambda i, j: (0, 0)),
                  pl.BlockSpec((4, d), lambda i, j: (0, 0)),
                  bspec, bspec, bspec, bspec,
                  pl.BlockSpec((N_BR, d, tn), lambda i, j: (0, 0, j)),
                  pl.BlockSpec((N_BR, 1, tn), lambda i, j: (0, 0, j)),
                  pl.BlockSpec((N_BR, BR_W, tn), lambda i, j: (0, 0, j))],
        out_specs=pl.BlockSpec((tm, tn), lambda i, j: (i, j)),
        out_shape=jax.ShapeDtypeStruct((m, d), BF16),
        scratch_shapes=[pltpu.VMEM((tm, d), BF16)],
        compiler_params=_params(("arbitrary", "arbitrary")),
        name="gated_merge",
    )(x, g.reshape(1, d), mod, *branches, wg, bg, wb)


def _resmm_body(a_ref, w_ref, res_ref, gate_ref, o_ref, *, n_lat, tm):
    i = pl.program_id(0)
    gate = _row_select(i, tm, n_lat, gate_ref[0:1, :], gate_ref[1:2, :])
    o_ref[...] = res_ref[...] + gate * jnp.dot(a_ref[...], w_ref[...], preferred_element_type=F32)


def _resmm(a, w, res, gate, *, n_lat, tm, tn):
    m, k = a.shape
    n = w.shape[1]
    body = functools.partial(_resmm_body, n_lat=n_lat, tm=tm)
    return pl.pallas_call(
        body, grid=(m // tm, n // tn),
        in_specs=[pl.BlockSpec((tm, k), lambda i, j: (i, 0)),
                  pl.BlockSpec((k, tn), lambda i, j: (0, j)),
                  pl.BlockSpec((tm, tn), lambda i, j: (i, j)),
                  pl.BlockSpec((2, tn), lambda i, j: (0, j))],
        out_specs=pl.BlockSpec((tm, tn), lambda i, j: (i, j)),
        out_shape=jax.ShapeDtypeStruct((m, n), F32),
        compiler_params=_params(("arbitrary", "arbitrary")),
        name="residual_matmul",
    )(a, w, res, gate)


P_RANKS = P_TOPK + 1
P_PAIRS = tuple((a, b) for a in range(P_RANKS) for b in range(P_RANKS) if (a + 1) * (b + 1) <= P_RANKS)


def _peer_topk_body(q_ref, knh_ref, khn_ref, ae_ref, be_ref, t_ref, cur_scr, top_scr, cand_scr, sum_scr, *, tt):
    q = q_ref[...]
    hk = P_HEADS * P_KEYS
    s_hn = []
    for p in range(2):
        qp = q[:, p * hk:(p + 1) * hk]
        cur_scr[p] = _dot_nt(knh_ref[p], qp).reshape(P_KEYS, P_HEADS, tt)
        s_hn.append(_dot_nt(khn_ref[p], qp).reshape(P_HEADS, P_KEYS, tt))

    def rank(r, carry):
        for p in range(2):
            c = cur_scr[p]
            mx = jnp.max(c, axis=0)
            cur_scr[p] = jnp.where(c == mx[None], NEG, c)
            top_scr[p * P_RANKS + r] = mx
        return carry

    lax.fori_loop(0, P_RANKS, rank, 0)

    for n, (a, b) in enumerate(P_PAIRS):
        cand_scr[n] = top_scr[a] + top_scr[P_RANKS + b]

    def rank_sum(r, carry):
        c = cand_scr[...]
        mx = jnp.max(c, axis=0)
        cand_scr[...] = jnp.where(c == mx[None], NEG, c)
        sum_scr[r] = mx
        return carry

    lax.fori_loop(0, P_RANKS, rank_sum, 0)

    t0 = sum_scr[0]
    z = jnp.zeros_like(t0)
    for r in range(P_TOPK):
        z = z + jnp.exp(sum_scr[r] - t0)
    inv_z = 1.0 / z
    t_ref[...] = jnp.exp(0.5 * (sum_scr[P_TOPK - 1] + sum_scr[P_TOPK]) - t0) * inv_z
    mx0, mx1 = top_scr[0], top_scr[P_RANKS]
    ae_ref[...] = (jnp.exp(s_hn[0] - mx0[:, None, :]) * inv_z[:, None, :]).reshape(hk, tt)
    be_ref[...] = jnp.exp(s_hn[1] - mx1[:, None, :]).reshape(hk, tt)


def _peer_topk(q, knh, khn, *, tt):
    m = q.shape[0]
    hk = P_HEADS * P_KEYS
    body = functools.partial(_peer_topk_body, tt=tt)
    return pl.pallas_call(
        body, grid=(m // tt,),
        in_specs=[pl.BlockSpec((tt, 2 * hk), lambda i: (i, 0)),
                  pl.BlockSpec((2, hk, hk), lambda i: (0, 0, 0)),
                  pl.BlockSpec((2, hk, hk), lambda i: (0, 0, 0))],
        out_specs=[pl.BlockSpec((hk, tt), lambda i: (0, i)),
                   pl.BlockSpec((hk, tt), lambda i: (0, i)),
                   pl.BlockSpec((P_HEADS, tt), lambda i: (0, i))],
        out_shape=[jax.ShapeDtypeStruct((hk, m), F32), jax.ShapeDtypeStruct((hk, m), F32),
                   jax.ShapeDtypeStruct((P_HEADS, m), F32)],
        scratch_shapes=[pltpu.VMEM((2, P_KEYS, P_HEADS, tt), F32),
                        pltpu.VMEM((2 * P_RANKS, P_HEADS, tt), F32),
                        pltpu.VMEM((len(P_PAIRS), P_HEADS, tt), F32),
                        pltpu.VMEM((P_RANKS, P_HEADS, tt), F32)],
        compiler_params=_params(("arbitrary",)),
        name="peer_topk",
    )(q, knh, khn)


def _peer_dense_body(u_ref, ht_ref, vt_ref, ae_ref, be_ref, t_ref, acc_ref, p_scr, *, te):
    e = pl.program_id(1)

    @pl.when(e == 0)
    def _():
        acc_ref[...] = jnp.zeros(acc_ref.shape, F32)

    act = jnp.dot(u_ref[...], ht_ref[...], preferred_element_type=F32)
    act = 0.5 * act * (1.0 + lax.erf(act * math.sqrt(0.5)))
    n_i = te // P_KEYS
    for ii in range(n_i):
        i = e * n_i + ii
        w = None
        for h in range(P_HEADS):
            a_row = ae_ref[pl.ds(h * P_KEYS + i, 1), :]
            prod = a_row * be_ref[h * P_KEYS:(h + 1) * P_KEYS, :]
            term = jnp.where(prod >= t_ref[h:h + 1, :], prod, 0.0)
            w = term if w is None else w + term
        p_scr[ii * P_KEYS:(ii + 1) * P_KEYS, :] = (w * act[ii * P_KEYS:(ii + 1) * P_KEYS, :]).astype(BF16)
    acc_ref[...] += jnp.dot(vt_ref[...], p_scr[...], preferred_element_type=F32)


def _peer_dense(u, ht, vt, ae, be, thr, *, tm, te):
    n_exp, d = u.shape
    m = ht.shape[1]
    hk = P_HEADS * P_KEYS
    body = functools.partial(_peer_dense_body, te=te)
    return pl.pallas_call(
        body, grid=(m // tm, n_exp // te),
        in_specs=[pl.BlockSpec((te, d), lambda t, e: (e, 0)),
                  pl.BlockSpec((d, tm), lambda t, e: (0, t)),
                  pl.BlockSpec((d, te), lambda t, e: (0, e)),
                  pl.BlockSpec((hk, tm), lambda t, e: (0, t)),
                  pl.BlockSpec((hk, tm), lambda t, e: (0, t)),
                  pl.BlockSpec((P_HEADS, tm), lambda t, e: (0, t))],
        out_specs=pl.BlockSpec((d, tm), lambda t, e: (0, t)),
        out_shape=jax.ShapeDtypeStruct((d, m), F32),
        scratch_shapes=[pltpu.VMEM((te, tm), BF16)],
        compiler_params=_params(("arbitrary", "arbitrary")),
        name="peer_dense",
    )(u, ht, vt, ae, be, thr)


def _peer_out_body(x_ref, at_ref, gate_ref, o_ref, *, n_lat, tm):
    gate = _row_select(pl.program_id(0), tm, n_lat, gate_ref[0:1, :], gate_ref[1:2, :])
    o_ref[...] = x_ref[...] + gate * at_ref[...].T


def _peer_out(x, acc_t, gate, *, n_lat, tm):
    m, d = x.shape
    body = functools.partial(_peer_out_body, n_lat=n_lat, tm=tm)
    return pl.pallas_call(
        body, grid=(m // tm,),
        in_specs=[pl.BlockSpec((tm, d), lambda i: (i, 0)),
                  pl.BlockSpec((d, tm), lambda i: (0, i)),
                  pl.BlockSpec((2, d), lambda i: (0, 0))],
        out_specs=pl.BlockSpec((tm, d), lambda i: (i, 0)),
        out_shape=jax.ShapeDtypeStruct((m, d), F32),
        compiler_params=_params(("arbitrary",)),
        name="peer_residual",
    )(x, acc_t, gate)


def _final_norm_body(x_ref, g_ref, o_ref):
    x = x_ref[...]
    o_ref[...] = x * lax.rsqrt(jnp.mean(x * x, axis=-1, keepdims=True) + EPS) * g_ref[...]


def _final_norm(x, g, *, n_rows, tm):
    d = x.shape[1]
    return pl.pallas_call(
        _final_norm_body, grid=(n_rows // tm,),
        in_specs=[pl.BlockSpec((tm, d), lambda i: (i, 0)), pl.BlockSpec((1, d), lambda i: (0, 0))],
        out_specs=pl.BlockSpec((tm, d), lambda i: (i, 0)),
        out_shape=jax.ShapeDtypeStruct((n_rows, d), F32),
        compiler_params=_params(("arbitrary",)),
        name="final_norm",
    )(x, g.reshape(1, d))


def _rope_tables(n_lat, n_ctx):
    t = jnp.arange(n_lat, dtype=jnp.int32)
    row = (t // GRID_W).astype(F32)
    col = (t % GRID_W).astype(F32)
    n_freq = ROPE_DIM // 4
    inv = ROPE_THETA ** (-jnp.arange(n_freq, dtype=F32) / n_freq)
    ang = jnp.concatenate([row[:, None] * inv, col[:, None] * inv], axis=-1)
    cos, sin = jnp.cos(ang), jnp.sin(ang)
    cos128 = jnp.concatenate([jnp.tile(cos, (1, 4)), jnp.ones((n_ctx, LANE), F32)], axis=0)
    sin128 = jnp.concatenate([jnp.tile(jnp.concatenate([-sin, sin], axis=1), (1, 2)),
                              jnp.zeros((n_ctx, LANE), F32)], axis=0)
    return cos128, sin128


def _rope_flags(width, ranges):
    f = np.zeros((1, width), np.float32)
    for lo, hi in ranges:
        f[0, lo:hi] = 1.0
    return jnp.asarray(f)


def _pad_heads(w, heads, width, padded):
    k = w.shape[0]
    w = w.reshape(k, heads, width)
    return jnp.pad(w, ((0, 0), (0, 0), (0, padded - width))).reshape(k, heads * padded)


def kernel(x, c, ctx, c_ctx, ada_w, ada_b, norm_mix, norm_ffn, w_in, a_lam_q1, a_lam_k1, a_lam_q2, a_lam_k2,
           a_subln, b_qa_norm, b_w_uq, b_kva_norm, b_w_ukv, c_sink, d_rpb, w_branch, w_gate, b_gate, w_out,
           peer_wq, peer_keys, peer_u, peer_v, final_norm):
    assert x.shape[0] == 1 and ctx.shape[0] == 1
    n_lat, d = x.shape[1], x.shape[2]
    n_ctx = ctx.shape[1]
    depth = ada_w.shape[0]
    ntok = n_lat + n_ctx
    rows = n_lat // GRID_W
    assert n_lat % n_ctx == 0 and n_lat % (D_QROWS * GRID_W) == 0 and rows >= D_KROWS

    tm = _tile(ntok, 768)
    xs = jnp.concatenate([x[0], ctx[0]], axis=0)
    cos128, sin128 = _rope_tables(n_lat, n_ctx)
    in_flags = _rope_flags(IN_PAD, [(COL_A_Q, COL_A_V), (COL_B_KPE, COL_B_KPE + B_ROPE), (COL_C_Q, COL_C_V)])
    uq_flags = _rope_flags(B_HEADS * 2 * LANE, [(h * 2 * LANE + B_NOPE, h * 2 * LANE + B_NOPE + B_ROPE)
                                                for h in range(B_HEADS)])

    c8 = jnp.zeros((8, d), F32).at[0].set(c[0]).at[1].set(c_ctx)
    mods = _ada(c8, ada_w, ada_b)

    b_scale2 = (B_NOPE + B_ROPE) ** -0.5 * LOG2E
    ks = np.ones((1, IN_PAD), np.float32)
    ks[0, COL_A_K:COL_A_V] = A_QK ** -0.5 * LOG2E
    ks[0, COL_B_KPE:COL_B_KPE + B_ROPE] = b_scale2
    ks[0, COL_C_K:COL_C_V] = C_HD ** -0.5 * LOG2E
    key_scale = jnp.asarray(ks)

    eye = jnp.eye(P_HEADS, dtype=F32)
    for l in range(depth):
        lam_init = 0.8 - 0.6 * math.exp(-0.3 * l)
        mod6 = mods[l, :2].reshape(2, 6, d)
        mod1 = jnp.concatenate([mod6[:, 0], mod6[:, 1]], axis=0)
        mod2 = jnp.concatenate([mod6[:, 3], mod6[:, 4]], axis=0)
        gate1, gate2 = mod6[:, 2], mod6[:, 5]

        w = w_in[l]
        k_pe_end = COL_B_KPE + B_ROPE
        w_in_p = jnp.concatenate([w[:, :k_pe_end], jnp.zeros((d, COL_C_Q - k_pe_end), F32), w[:, k_pe_end:],
                                  jnp.zeros((d, IN_PAD - (COL_D_V + D_HEADS * D_HD)), F32)], axis=1)
        w_in_p = (w_in_p * key_scale).astype(BF16)
        w_uq_p = _pad_heads(b_w_uq[l], B_HEADS, B_NOPE + B_ROPE, 2 * LANE).astype(BF16)
        w_ukv = b_w_ukv[l].reshape(B_KVL, B_HEADS, B_NOPE + B_V)
        w_ukv_p = jnp.concatenate([w_ukv[:, :, :B_NOPE].reshape(B_KVL, -1) * b_scale2,
                                   w_ukv[:, :, B_NOPE:].reshape(B_KVL, -1)], axis=1).astype(BF16)
        wq_p = peer_wq[l].reshape(d, P_HEADS, 2, P_KD // 2).transpose(0, 2, 1, 3).reshape(d, -1).astype(BF16)
        keys = peer_keys[l]
        knh = jnp.stack([jnp.einsum('hnd,hg->nhgd', keys[:, p], eye).reshape(P_HEADS * P_KEYS, -1)
                         for p in range(2)]).astype(BF16)
        khn = jnp.stack([jnp.einsum('hnd,hg->hngd', keys[:, p], eye).reshape(P_HEADS * P_KEYS, -1)
                         for p in range(2)]).astype(BF16)
        u_b = peer_u[l].astype(BF16)
        vt_b = peer_v[l].T.astype(BF16)

        p = _nmm(xs, 0, d, norm_mix[l], mod1, w_in_p, n_lat=n_lat, tm=tm, tn=_tile(IN_PAD, 1024),
                 out_dtype=BF16, rope=(cos128, sin128, in_flags))
        qb = _nmm(p, COL_B_QL // B_QL, B_QL, b_qa_norm[l], None, w_uq_p, n_lat=n_lat, tm=tm, tn=w_uq_p.shape[1],
                  out_dtype=BF16, rope=(cos128, sin128, uq_flags))
        kvb = _nmm(p, COL_B_KVL // B_KVL, B_KVL, b_kva_norm[l], None, w_ukv_p, n_lat=n_lat, tm=tm,
                   tn=w_ukv_p.shape[1], out_dtype=BF16)
        lam_p = jnp.stack([a_lam_q1[l], a_lam_k1[l], a_lam_q2[l], a_lam_k2[l]])
        subg = a_subln[l].reshape(1, A_V)
        bias_tbl = _dna_bias_table(d_rpb[l].astype(F32), rows)

        def attend(q_row0, n_q, kv_row0, nk, tq, tk):
            oa = _flash("diff", p, COL_A_Q, LANE, [(p, COL_A_K, 1)], p, COL_A_V, A_V, heads=A_HEADS,
                        q_row0=q_row0, n_q=n_q, kv_row0=kv_row0, nk=nk, tq=tq, tk=tk, scale=None,
                        out_cols=A_HEADS * A_V, extra=(lam_p, subg), lam_init=lam_init)
            ob = _flash("cat", qb, 0, 2 * LANE, [(kvb, 0, 1), (p, COL_B_KPE, 0)], kvb, B_HEADS * B_NOPE, B_V,
                        heads=B_HEADS, q_row0=q_row0, n_q=n_q, kv_row0=kv_row0, nk=nk, tq=tq, tk=tk,
                        scale=None, out_cols=B_HEADS * B_V)
            return oa, ob

        tk = _tile(ntok, 768)
        oa, ob = attend(0, n_lat, 0, ntok, _tile(n_lat, 512), tk)
        oc = _cwin(p, c_sink[l], n_lat=n_lat, n_ctx=n_ctx, q_row0=0, n_q=n_lat, with_band=True)
        od = _dna(p, bias_tbl, n_lat=n_lat, n_ctx=n_ctx)
        if l < depth - 1:
            oac, obc = attend(n_lat, n_ctx, n_lat, n_ctx, n_ctx, n_ctx)
            occ = _cwin(p, c_sink[l], n_lat=n_lat, n_ctx=n_ctx, q_row0=n_lat, n_q=n_ctx, with_band=False)
            odc = _flash("plain", p, COL_D_Q, D_HD, [(p, COL_D_K, 1)], p, COL_D_V, D_HD, heads=D_HEADS,
                         q_row0=n_lat, n_q=n_ctx, kv_row0=n_lat, nk=n_ctx, tq=n_ctx, tk=n_ctx,
                         scale=D_HD ** -0.5, out_cols=D_HEADS * D_HD)
            branches = [jnp.concatenate([a, b], axis=0) for a, b in ((oa, oac), (ob, obc), (oc, occ), (od, odc))]
        else:
            branches = [jnp.pad(a, ((0, n_ctx), (0, 0))) for a in (oa, ob, oc, od)]
        merged = _merge(xs, norm_mix[l], mod1, branches, w_gate[l].astype(BF16), b_gate[l].reshape(N_BR, 1, d),
                        w_branch[l].astype(BF16), n_lat=n_lat, tm=tm, tn=_tile(d, 256))
        xs = _resmm(merged, w_out[l].astype(BF16), xs, gate1, n_lat=n_lat, tm=tm, tn=_tile(d, 1024))

        qp, h_t = _nmm(xs, 0, d, norm_ffn[l], mod2, wq_p, n_lat=n_lat, tm=tm, tn=_tile(d, 1024), out_dtype=BF16,
                       want_ht=True)
        ae, be, thr = _peer_topk(qp, knh, khn, tt=_tile(ntok, 256))
        acc_t = _peer_dense(u_b, h_t, vt_b, ae, be, thr, tm=tm, te=1024)
        xs = _peer_out(xs, acc_t, gate2, n_lat=n_lat, tm=_tile(ntok, 256))

    out = _final_norm(xs, final_norm, n_rows=n_lat, tm=_tile(n_lat, 512))
    return out[None]
```
